```python
import jax, jax.numpy as jnp
from jax import lax
import numpy as np

D_MODEL = 1024
BATCH = 8
SEQ = 2048
DEPTH = 1

N_Q_HEADS = 16
N_KV_HEADS = 4
HEAD_DIM = 64
WINDOW = 128
ATTN_BLOCK = WINDOW
SSD_EXPAND = 2
D_INNER = SSD_EXPAND * D_MODEL
SSD_HEAD_DIM = 64
N_SSD_HEADS = D_INNER // SSD_HEAD_DIM
N_SSD_GROUPS = 4
D_STATE = 128
SSD_CONV = 4
CHUNK = 128
D_FF = 2816
FFN_CONV = 3
EPS = 1e-5
NEG = -1e30

Q_DIM = N_Q_HEADS * HEAD_DIM
KV_DIM = N_KV_HEADS * HEAD_DIM
BC_DIM = N_SSD_GROUPS * D_STATE
XBC_DIM = D_INNER + 2 * BC_DIM
IN_SPLITS = (Q_DIM, KV_DIM, KV_DIM, D_INNER, XBC_DIM, N_SSD_HEADS, D_MODEL, D_MODEL)
IN_DIM = sum(IN_SPLITS)

kernel_name = "hybrid_swa_sink_ssd_convffn"


def _split(t, sizes):
    idx = np.cumsum(np.array(sizes))[:-1].tolist()
    return jnp.split(t, idx, axis=-1)


def rmsnorm(x, w):
    xf = x.astype(jnp.float32)
    y = xf * lax.rsqrt(jnp.mean(xf * xf, axis=-1, keepdims=True) + EPS)
    return (y * w.astype(jnp.float32)).astype(x.dtype)


def causal_dwconv(x, w, b):
    K = w.shape[0]
    S = x.shape[1]
    xp = jnp.pad(x, ((0, 0), (K - 1, 0), (0, 0)))
    y = xp[:, 0:S] * w[0]
    for k in range(1, K):
        y = y + xp[:, k:k + S] * w[k]
    return y + b


def banded_sink_attention(q, k, v, sinks):
    Bsz, S, _ = q.shape
    W = ATTN_BLOCK
    nb = S // W
    G = N_Q_HEADS // N_KV_HEADS
    qb = q.reshape(Bsz, nb, W, N_KV_HEADS, G, HEAD_DIM)

    def band(t):
        t = t.reshape(Bsz, S, N_KV_HEADS, HEAD_DIM)
        tp = jnp.pad(t, ((0, 0), (W, 0), (0, 0), (0, 0)))
        prev = tp[:, :S].reshape(Bsz, nb, W, N_KV_HEADS, HEAD_DIM)
        cur = t.reshape(Bsz, nb, W, N_KV_HEADS, HEAD_DIM)
        return jnp.concatenate([prev, cur], axis=2)

    kb, vb = band(k), band(v)
    scores = jnp.einsum('bnqhgd,bnshd->bnhgqs', qb, kb).astype(jnp.float32) * (HEAD_DIM ** -0.5)
    qi = jnp.arange(W)[:, None]
    si = jnp.arange(2 * W)[None, :]
    dist = W + qi - si
    kpos = jnp.arange(nb)[:, None, None] * W - W + si[None]
    valid = (dist >= 0)[None] & (dist < WINDOW)[None] & (kpos >= 0)
    valid = valid[None, :, None, None]
    scores = jnp.where(valid, scores, NEG)
    sink = sinks.astype(jnp.float32).reshape(N_KV_HEADS, G)[None, None, :, :, None]
    m = jnp.maximum(scores.max(axis=-1), sink)
    p = jnp.where(valid, jnp.exp(scores - m[..., None]), 0.0)
    denom = p.sum(axis=-1) + jnp.exp(sink - m)
    probs = (p / denom[..., None]).astype(v.dtype)
    out = jnp.einsum('bnhgqs,bnshd->bnqhgd', probs, vb)
    return out.reshape(Bsz, S, Q_DIM)


def ssd_chunked(xs, dt, a_log, bmat, cmat, d_skip):
    Bsz, S, H, P = xs.shape
    G, N = bmat.shape[2], bmat.shape[3]
    J = H // G
    L = CHUNK
    nc = S // L
    A = -jnp.exp(a_log.astype(jnp.float32))
    dA = (dt * A).reshape(Bsz, nc, L, G, J)
    xf = xs.astype(jnp.float32)
    xc = (xf * dt[..., None]).reshape(Bsz, nc, L, G, J, P)
    bc = bmat.astype(jnp.float32).reshape(Bsz, nc, L, G, N)
    cc = cmat.astype(jnp.float32).reshape(Bsz, nc, L, G, N)
    a_cs = jnp.cumsum(dA, axis=2)
    seg = a_cs[:, :, :, None] - a_cs[:, :, None, :]
    causal = jnp.tril(jnp.ones((L, L), dtype=bool))[:, :, None, None]
    decay = jnp.where(causal, jnp.exp(jnp.where(causal, seg, 0.0)), 0.0)
    cb = jnp.einsum('bclgn,bcsgn->bclsg', cc, bc)
    y_diag = jnp.einsum('bclsgj,bcsgjp->bclgjp', cb[..., None] * decay, xc)
    decay_states = jnp.exp(a_cs[:, :, -1:] - a_cs)
    states = jnp.einsum('bclgn,bclgjp->bcgjpn', bc, xc * decay_states[..., None])
    chunk_decay = jnp.exp(a_cs[:, :, -1])

    def step(h, inp):
        st, dec = inp
        h_new = h * dec[..., None, None] + st
        return h_new, h

    h0 = jnp.zeros((Bsz, G, J, P, N), jnp.float32)
    _, prev = lax.scan(step, h0, (jnp.swapaxes(states, 0, 1), jnp.swapaxes(chunk_decay, 0, 1)))
    prev = jnp.swapaxes(prev, 0, 1)
    y_off = jnp.einsum('bclgn,bcgjpn->bclgjp', cc, prev) * jnp.exp(a_cs)[..., None]
    y = (y_diag + y_off).reshape(Bsz, S, H, P)
    return y + xf * d_skip.astype(jnp.float32)[:, None]


def setup_inputs(seed: int = 0) -> dict:
    key = jax.random.key(seed)
    ks = jax.random.split(key, 24)
    f32 = jnp.float32
    nrm = lambda k, shape, scale: jax.random.normal(k, shape, f32) * scale
    dt0 = jnp.exp(jax.random.uniform(ks[9], (DEPTH, N_SSD_HEADS), f32,
                                     jnp.log(0.001), jnp.log(0.1)))
    return {
        "x": nrm(ks[0], (BATCH, SEQ, D_MODEL), 1.0),
        "norm1_w": 1.0 + nrm(ks[1], (DEPTH, D_MODEL), 0.02),
        "w_in": nrm(ks[2], (DEPTH, D_MODEL, IN_DIM), D_MODEL ** -0.5),
        "b_gate": nrm(ks[3], (DEPTH, 2 * D_MODEL), 0.02),
        "attn_sinks": nrm(ks[4], (DEPTH, N_Q_HEADS), 0.5),
        "w_attn_o": nrm(ks[5], (DEPTH, Q_DIM, D_MODEL), Q_DIM ** -0.5),
        "ssd_conv_w": nrm(ks[6], (DEPTH, SSD_CONV, XBC_DIM), SSD_CONV ** -0.5),
        "ssd_conv_b": nrm(ks[7], (DEPTH, XBC_DIM), 0.02),
        "dt_bias": dt0 + jnp.log(-jnp.expm1(-dt0)),
        "a_log": jnp.log(jax.random.uniform(ks[10], (DEPTH, N_SSD_HEADS), f32, 1.0, 16.0)),
        "d_skip": 1.0 + nrm(ks[11], (DEPTH, N_SSD_HEADS), 0.02),
        "ssd_norm_w": 1.0 + nrm(ks[12], (DEPTH, D_INNER), 0.02),
        "w_ssd_o": nrm(ks[13], (DEPTH, D_INNER, D_MODEL), D_INNER ** -0.5),
        "w_out": nrm(ks[14], (DEPTH, D_MODEL, D_MODEL), D_MODEL ** -0.5),
        "norm2_w": 1.0 + nrm(ks[15], (DEPTH, D_MODEL), 0.02),
        "w_up": nrm(ks[16], (DEPTH, D_MODEL, 2 * D_FF), D_MODEL ** -0.5),
        "ffn_conv_w": nrm(ks[17], (DEPTH, FFN_CONV, 2 * D_FF), FFN_CONV ** -0.5),
        "ffn_conv_b": nrm(ks[18], (DEPTH, 2 * D_FF), 0.02),
        "w_down": nrm(ks[19], (DEPTH, D_FF, D_MODEL), D_FF ** -0.5),
        "final_norm_w": 1.0 + nrm(ks[20], (D_MODEL,), 0.02),
    }


def reference(x, norm1_w, w_in, b_gate, attn_sinks, w_attn_o, ssd_conv_w, ssd_conv_b, dt_bias,
              a_log, d_skip, ssd_norm_w, w_ssd_o, w_out, norm2_w, w_up, ffn_conv_w, ffn_conv_b,
              w_down, final_norm_w):
    Bsz, S, _ = x.shape
    h = x
    for layer in range(DEPTH):
        xn = rmsnorm(h, norm1_w[layer])
        proj = xn @ w_in[layer]
        q, k, v, z, xbc, dt_raw, ga_raw, gs_raw = _split(proj, IN_SPLITS)
        ba, bs = _split(b_gate[layer], (D_MODEL, D_MODEL))
        gate_a = jax.nn.sigmoid(ga_raw + ba)
        gate_s = jax.nn.sigmoid(gs_raw + bs)
        attn = banded_sink_attention(q, k, v, attn_sinks[layer]) @ w_attn_o[layer]
        xbc = jax.nn.silu(causal_dwconv(xbc, ssd_conv_w[layer], ssd_conv_b[layer]))
        xs, bm, cm = _split(xbc, (D_INNER, BC_DIM, BC_DIM))
        dt = jax.nn.softplus(dt_raw.astype(jnp.float32) + dt_bias[layer].astype(jnp.float32))
        y = ssd_chunked(xs.reshape(Bsz, S, N_SSD_HEADS, SSD_HEAD_DIM), dt, a_log[layer],
                        bm.reshape(Bsz, S, N_SSD_GROUPS, D_STATE),
                        cm.reshape(Bsz, S, N_SSD_GROUPS, D_STATE), d_skip[layer])
        y = y.reshape(Bsz, S, D_INNER) * jax.nn.silu(z.astype(jnp.float32))
        yg = y.reshape(Bsz, S, N_SSD_GROUPS, D_INNER // N_SSD_GROUPS)
        yg = yg * lax.rsqrt(jnp.mean(yg * yg, axis=-1, keepdims=True) + EPS)
        y = (yg.reshape(Bsz, S, D_INNER) * ssd_norm_w[layer].astype(jnp.float32)).astype(x.dtype)
        ssd_out = y @ w_ssd_o[layer]
        mix = (gate_a * attn + gate_s * ssd_out) @ w_out[layer]
        h = h + mix.astype(h.dtype)
        hn = rmsnorm(h, norm2_w[layer])
        u = causal_dwconv(hn @ w_up[layer], ffn_conv_w[layer], ffn_conv_b[layer])
        val, gt = _split(u, (D_FF, D_FF))
        h = h + ((jax.nn.silu(gt) * val) @ w_down[layer]).astype(h.dtype)
    return rmsnorm(h, final_norm_w)
```

```python
import functools

import jax
import jax.numpy as jnp
from jax import lax
from jax.experimental import pallas as pl
from jax.experimental.pallas import tpu as pltpu

F32 = jnp.float32
BF16 = jnp.bfloat16

D_MODEL = 1024
N_Q_HEADS = 16
N_KV_HEADS = 4
Q_PER_KV = N_Q_HEADS // N_KV_HEADS
HEAD_DIM = 64
WINDOW = 128
D_INNER = 2048
SSD_HEAD_DIM = 64
N_SSD_HEADS = 32
N_SSD_GROUPS = 4
HEADS_PER_GROUP = N_SSD_HEADS // N_SSD_GROUPS
D_STATE = 128
SSD_CONV = 4
CHUNK = 128
D_FF = 2816
FFN_CONV = 3
EPS = 1e-5
NEG = -1e30
Q_DIM = N_Q_HEADS * HEAD_DIM
KV_DIM = N_KV_HEADS * HEAD_DIM
BC_DIM = N_SSD_GROUPS * D_STATE
GROUP_W = D_INNER // N_SSD_GROUPS

LANES = 128
SUBLANES = 8
VMEM_LIMIT_BYTES = 56 * 1024 * 1024

OFF_Z = 0
OFF_XS = OFF_Z + D_INNER
OFF_Q = OFF_XS + D_INNER
OFF_GA = OFF_Q + Q_DIM
OFF_GS = OFF_GA + D_MODEL
OFF_B = OFF_GS + D_MODEL
OFF_C = OFF_B + BC_DIM
OFF_K = OFF_C + BC_DIM
OFF_V = OFF_K + KV_DIM
PROJ_W = OFF_V + KV_DIM

IN_TM = 1024
IN_TN = 512
MERGE_TM = 512
FFN_TM = 512
FFN_CW = 256
FFN_NCHUNK = D_FF // FFN_CW


def _silu(x):
    return x * (1.0 / (1.0 + jnp.exp(-x)))


def _sigmoid(x):
    return 1.0 / (1.0 + jnp.exp(-x))


def _softplus(x):
    return jnp.maximum(x, 0.0) + jnp.log(1.0 + jnp.exp(-jnp.abs(x)))


def _split3(x):
    hi = x.astype(BF16)
    r1 = x - hi.astype(F32)
    mid = r1.astype(BF16)
    lo = (r1 - mid.astype(F32)).astype(BF16)
    return hi, mid, lo


def _inproj_kernel(x_ref, nw_ref, w_ref, wdt_ref, wdtT_ref, dtb_ref, dtbT_ref,
                   proj_ref, dt_ref, dtT_ref, xn_ref):
    @pl.when(pl.program_id(1) == 0)
    def _():
        x = x_ref[...]
        ms = jnp.mean(x * x, axis=-1, keepdims=True)
        xn = (x * lax.rsqrt(ms + EPS) * nw_ref[...]).astype(BF16)
        xn_ref[...] = xn
        dt_raw = jnp.dot(xn, wdt_ref[...], preferred_element_type=F32)
        dt_ref[...] = _softplus(dt_raw + dtb_ref[...])
        dtT_raw = lax.dot_general(wdtT_ref[...], xn, (((1,), (1,)), ((), ())),
                                  preferred_element_type=F32)
        dtT_ref[...] = _softplus(dtT_raw + dtbT_ref[...])

    proj_ref[...] = jnp.dot(xn_ref[...], w_ref[...], preferred_element_type=F32).astype(BF16)


def _inproj(x2d, norm_w, w_main, w_dt, w_dtT, dt_bias_row, dt_bias_col):
    T = x2d.shape[0]
    tm = min(IN_TM, T)
    grid = (T // tm, PROJ_W // IN_TN)
    return pl.pallas_call(
        _inproj_kernel,
        grid=grid,
        in_specs=[
            pl.BlockSpec((tm, D_MODEL), lambda i, j: (i, 0)),
            pl.BlockSpec((1, D_MODEL), lambda i, j: (0, 0)),
            pl.BlockSpec((D_MODEL, IN_TN), lambda i, j: (0, j)),
            pl.BlockSpec((D_MODEL, LANES), lambda i, j: (0, 0)),
            pl.BlockSpec((N_SSD_HEADS, D_MODEL), lambda i, j: (0, 0)),
            pl.BlockSpec((1, LANES), lambda i, j: (0, 0)),
            pl.BlockSpec((N_SSD_HEADS, 1), lambda i, j: (0, 0)),
        ],
        out_specs=[
            pl.BlockSpec((tm, IN_TN), lambda i, j: (i, j)),
            pl.BlockSpec((tm, LANES), lambda i, j: (i, 0)),
            pl.BlockSpec((N_SSD_HEADS, tm), lambda i, j: (0, i)),
        ],
        out_shape=[
            jax.ShapeDtypeStruct((T, PROJ_W), BF16),
            jax.ShapeDtypeStruct((T, LANES), F32),
            jax.ShapeDtypeStruct((N_SSD_HEADS, T), F32),
        ],
        scratch_shapes=[pltpu.VMEM((tm, D_MODEL), BF16)],
        compiler_params=pltpu.CompilerParams(
            dimension_semantics=("arbitrary", "arbitrary"), vmem_limit_bytes=VMEM_LIMIT_BYTES),
        name="inproj",
    )(x2d, norm_w, w_main, w_dt, w_dtT, dt_bias_row, dt_bias_col)


def _attn_kernel(sinks_ref, q_ref, kp_ref, kc_ref, vp_ref, vc_ref, o_ref):
    n = pl.program_id(1)
    W = WINDOW
    qi = lax.broadcasted_iota(jnp.int32, (W, 2 * W), 0)
    si = lax.broadcasted_iota(jnp.int32, (W, 2 * W), 1)
    valid = jnp.logical_or(jnp.logical_and(jnp.logical_and(si < W, si > qi), n > 0),
                           jnp.logical_and(si >= W, si - W <= qi))
    q = q_ref[...] * (HEAD_DIM ** -0.5)
    outs = []
    for h in range(N_KV_HEADS):
        cs = slice(h * HEAD_DIM, (h + 1) * HEAD_DIM)
        kh = jnp.concatenate([kp_ref[:, cs], kc_ref[:, cs]], axis=0)
        vh = jnp.concatenate([vp_ref[:, cs], vc_ref[:, cs]], axis=0)
        for g in range(Q_PER_KV):
            head = h * Q_PER_KV + g
            qg = q[:, head * HEAD_DIM:(head + 1) * HEAD_DIM]
            s = lax.dot_general(qg, kh, (((1,), (1,)), ((), ())), preferred_element_type=F32)
            s = jnp.where(valid, s, NEG)
            sink = sinks_ref[head]
            m = jnp.maximum(jnp.max(s, axis=-1, keepdims=True), sink)
            p = jnp.exp(s - m)
            denom = jnp.sum(p, axis=-1, keepdims=True) + jnp.exp(sink - m)
            o = jnp.dot(p.astype(BF16), vh, preferred_element_type=F32)
            outs.append(o * (1.0 / denom))
    o_ref[...] = jnp.concatenate(outs, axis=-1).astype(BF16)


def _attention(proj, sinks, batch, seq):
    nb = seq // WINDOW
    T = batch * seq
    row = lambda b, n: b * nb + n
    prow = lambda b, n: b * nb + jnp.maximum(n - 1, 0)
    return pl.pallas_call(
        _attn_kernel,
        grid=(batch, nb),
        in_specs=[
            pl.BlockSpec(memory_space=pltpu.SMEM),
            pl.BlockSpec((WINDOW, Q_DIM), lambda b, n: (row(b, n), OFF_Q // Q_DIM)),
            pl.BlockSpec((WINDOW, KV_DIM), lambda b, n: (prow(b, n), OFF_K // KV_DIM)),
            pl.BlockSpec((WINDOW, KV_DIM), lambda b, n: (row(b, n), OFF_K // KV_DIM)),
            pl.BlockSpec((WINDOW, KV_DIM), lambda b, n: (prow(b, n), OFF_V // KV_DIM)),
            pl.BlockSpec((WINDOW, KV_DIM), lambda b, n: (row(b, n), OFF_V // KV_DIM)),
        ],
        out_specs=pl.BlockSpec((WINDOW, Q_DIM), lambda b, n: (row(b, n), 0)),
        out_shape=jax.ShapeDtypeStruct((T, Q_DIM), BF16),
        compiler_params=pltpu.CompilerParams(
            dimension_semantics=("arbitrary", "arbitrary"), vmem_limit_bytes=VMEM_LIMIT_BYTES),
        name="swa_attention",
    )(sinks, proj, proj, proj, proj, proj)


def _causal_conv(buf_ref, raw, w_ref, b_ref, taps, first):
    L = raw.shape[0]
    @pl.when(first)
    def _():
        buf_ref[0:SUBLANES, :] = jnp.zeros((SUBLANES, buf_ref.shape[1]), F32)
    buf_ref[SUBLANES:SUBLANES + L, :] = raw
    acc = raw * w_ref[taps - 1:taps, :] + b_ref[...]
    for k in range(taps - 1):
        off = SUBLANES - (taps - 1) + k
        acc = acc + buf_ref[off:off + L, :] * w_ref[k:k + 1, :]
    buf_ref[0:SUBLANES, :] = raw[L - SUBLANES:L, :]
    return acc


def _pair_bcast(cols, h0, lane_lt_half, shape):
    a = jnp.broadcast_to(cols[:, h0:h0 + 1], shape)
    b = jnp.broadcast_to(cols[:, h0 + 1:h0 + 2], shape)
    return jnp.where(lane_lt_half, a, b)


def _ssd_kernel(z_ref, xs_ref, b_ref, c_ref, dt_ref, dtT_ref,
                cwx_ref, cbx_ref, cwb_ref, cbb_ref, cwc_ref, cbc_ref,
                alog_ref, alogT_ref, dskip_ref, nw_ref,
                y_ref,
                state_ref, bufx_ref, bufb_ref, bufc_ref):
    n = pl.program_id(1)
    first = n == 0
    L = CHUNK

    @pl.when(first)
    def _():
        state_ref[...] = jnp.zeros(state_ref.shape, F32)

    xs = _silu(_causal_conv(bufx_ref, xs_ref[...].astype(F32), cwx_ref, cbx_ref, SSD_CONV, first))
    bm = _silu(_causal_conv(bufb_ref, b_ref[...].astype(F32), cwb_ref, cbb_ref, SSD_CONV, first))
    cm = _silu(_causal_conv(bufc_ref, c_ref[...].astype(F32), cwc_ref, cbc_ref, SSD_CONV, first))

    dt = dt_ref[...]
    dtT = dtT_ref[...]
    dA = dt * (-jnp.exp(alog_ref[...]))
    dAT = dtT * (-jnp.exp(alogT_ref[...]))

    ri = lax.broadcasted_iota(jnp.int32, (L, L), 0)
    ci = lax.broadcasted_iota(jnp.int32, (L, L), 1)
    causal = ci <= ri
    tri = causal.astype(BF16)
    triT = (ri <= ci).astype(BF16)
    a_cs = sum(jnp.dot(tri, p, preferred_element_type=F32) for p in _split3(dA))
    a_csT = sum(jnp.dot(p, triT, preferred_element_type=F32) for p in _split3(dAT))

    a_last = a_cs[L - 1:L, :]
    ea = jnp.exp(a_cs)
    w_state = dt * jnp.exp(a_last - a_cs)
    cd = jnp.exp(a_last)

    lane_lt_half = lax.broadcasted_iota(jnp.int32, (L, LANES), 1) < SSD_HEAD_DIM
    lane_lt_half_row = lane_lt_half[0:1, :]

    xs_bf = xs.astype(BF16)
    y_groups = []
    for g in range(N_SSD_GROUPS):
        bg = bm[:, g * D_STATE:(g + 1) * D_STATE]
        cg = cm[:, g * D_STATE:(g + 1) * D_STATE]
        bg_bf = bg.astype(BF16)
        cg_bf = cg.astype(BF16)
        cb = lax.dot_general(cg_bf, bg_bf, (((1,), (1,)), ((), ())), preferred_element_type=F32)
        prev = state_ref[g]
        y_off = jnp.dot(cg_bf, prev.astype(BF16), preferred_element_type=F32)
        y_pairs, xw_pairs, cd_pairs = [], [], []
        for jp in range(HEADS_PER_GROUP // 2):
            h0 = g * HEADS_PER_GROUP + 2 * jp
            col = slice((h0 // 2) * LANES, (h0 // 2 + 1) * LANES)
            xs_pair = xs[:, col]
            xs_pair_bf = xs_bf[:, col]
            yd = []
            for h in (h0, h0 + 1):
                seg = a_cs[:, h:h + 1] - a_csT[h:h + 1, :]
                m = cb * jnp.exp(jnp.where(causal, seg, NEG)) * dtT[h:h + 1, :]
                yd.append(jnp.dot(m.astype(BF16), xs_pair_bf, preferred_element_type=F32))
            y_diag = jnp.where(lane_lt_half, yd[0], yd[1])
            y_pairs.append(y_diag + y_off[:, jp * LANES:(jp + 1) * LANES]
                           * _pair_bcast(ea, h0, lane_lt_half, (L, LANES)))
            xw_pairs.append(xs_pair * _pair_bcast(w_state, h0, lane_lt_half, (L, LANES)))
            cd_pairs.append(_pair_bcast(cd, h0, lane_lt_half_row, (1, LANES)))
        xw = jnp.concatenate(xw_pairs, axis=-1).astype(BF16)
        new_states = jnp.dot(bg.T.astype(BF16), xw, preferred_element_type=F32)
        state_ref[g] = prev * jnp.concatenate(cd_pairs, axis=-1) + new_states
        y_groups.append(jnp.concatenate(y_pairs, axis=-1))

    for g in range(N_SSD_GROUPS):
        col = slice(g * GROUP_W, (g + 1) * GROUP_W)
        y = y_groups[g] + xs[:, col] * dskip_ref[:, col]
        y = y * _silu(z_ref[:, col].astype(F32))
        ms = jnp.mean(y * y, axis=-1, keepdims=True)
        y_ref[:, col] = (y * lax.rsqrt(ms + EPS) * nw_ref[:, col]).astype(BF16)


def _ssd(proj, dt, dtT, conv_w, conv_b, a_log, d_skip, norm_w, batch, seq):
    nc = seq // CHUNK
    T = batch * seq
    row = lambda b, n: b * nc + n
    full = lambda shape: pl.BlockSpec(shape, lambda b, n: (0,) * len(shape))
    cwx, cwb, cwc = conv_w[:, :D_INNER], conv_w[:, D_INNER:D_INNER + BC_DIM], conv_w[:, D_INNER + BC_DIM:]
    cbx, cbb, cbc = conv_b[:, :D_INNER], conv_b[:, D_INNER:D_INNER + BC_DIM], conv_b[:, D_INNER + BC_DIM:]
    alog_row = jnp.pad(a_log.reshape(1, N_SSD_HEADS), ((0, 0), (0, LANES - N_SSD_HEADS)))
    alog_col = a_log.reshape(N_SSD_HEADS, 1)
    dskip_row = jnp.repeat(d_skip, SSD_HEAD_DIM).reshape(1, D_INNER)
    return pl.pallas_call(
        _ssd_kernel,
        grid=(batch, nc),
        in_specs=[
            pl.BlockSpec((CHUNK, D_INNER), lambda b, n: (row(b, n), OFF_Z // D_INNER)),
            pl.BlockSpec((CHUNK, D_INNER), lambda b, n: (row(b, n), OFF_XS // D_INNER)),
            pl.BlockSpec((CHUNK, BC_DIM), lambda b, n: (row(b, n), OFF_B // BC_DIM)),
            pl.BlockSpec((CHUNK, BC_DIM), lambda b, n: (row(b, n), OFF_C // BC_DIM)),
            pl.BlockSpec((CHUNK, LANES), lambda b, n: (row(b, n), 0)),
            pl.BlockSpec((N_SSD_HEADS, CHUNK), lambda b, n: (0, row(b, n))),
            full((SSD_CONV, D_INNER)), full((1, D_INNER)),
            full((SSD_CONV, BC_DIM)), full((1, BC_DIM)),
            full((SSD_CONV, BC_DIM)), full((1, BC_DIM)),
            full((1, LANES)), full((N_SSD_HEADS, 1)),
            full((1, D_INNER)), full((1, D_INNER)),
        ],
        out_specs=pl.BlockSpec((CHUNK, D_INNER), lambda b, n: (row(b, n), 0)),
        out_shape=jax.ShapeDtypeStruct((T, D_INNER), BF16),
        scratch_shapes=[
            pltpu.VMEM((N_SSD_GROUPS, D_STATE, GROUP_W), F32),
            pltpu.VMEM((SUBLANES + CHUNK, D_INNER), F32),
            pltpu.VMEM((SUBLANES + CHUNK, BC_DIM), F32),
            pltpu.VMEM((SUBLANES + CHUNK, BC_DIM), F32),
        ],
        compiler_params=pltpu.CompilerParams(
            dimension_semantics=("arbitrary", "arbitrary"), vmem_limit_bytes=VMEM_LIMIT_BYTES),
        name="ssd_mixer",
    )(proj, proj, proj, proj, dt, dtT, cwx, cbx, cwb, cbb, cwc, cbc,
      alog_row, alog_col, dskip_row, norm_w)


def _merge_kernel(x_ref, attn_ref, y_ref, ga_ref, gs_ref, bg_ref, wa_ref, ws_ref, wo_ref, h_ref):
    attn = jnp.dot(attn_ref[...], wa_ref[...], preferred_element_type=F32)
    ssd = jnp.dot(y_ref[...], ws_ref[...], preferred_element_type=F32)
    gate_a = _sigmoid(ga_ref[...].astype(F32) + bg_ref[:, :D_MODEL])
    gate_s = _sigmoid(gs_ref[...].astype(F32) + bg_ref[:, D_MODEL:])
    mixed = (gate_a * attn + gate_s * ssd).astype(BF16)
    h_ref[...] = x_ref[...] + jnp.dot(mixed, wo_ref[...], preferred_element_type=F32)


def _merge(x2d, attn, y, proj, b_gate, w_attn_o, w_ssd_o, w_out):
    T = x2d.shape[0]
    tm = min(MERGE_TM, T)
    full = lambda shape: pl.BlockSpec(shape, lambda i: (0,) * len(shape))
    return pl.pallas_call(
        _merge_kernel,
        grid=(T // tm,),
        in_specs=[
            pl.BlockSpec((tm, D_MODEL), lambda i: (i, 0)),
            pl.BlockSpec((tm, Q_DIM), lambda i: (i, 0)),
            pl.BlockSpec((tm, D_INNER), lambda i: (i, 0)),
            pl.BlockSpec((tm, D_MODEL), lambda i: (i, OFF_GA // D_MODEL)),
            pl.BlockSpec((tm, D_MODEL), lambda i: (i, OFF_GS // D_MODEL)),
            full((1, 2 * D_MODEL)),
            full((Q_DIM, D_MODEL)), full((D_INNER, D_MODEL)), full((D_MODEL, D_MODEL)),
        ],
        out_specs=pl.BlockSpec((tm, D_MODEL), lambda i: (i, 0)),
        out_shape=jax.ShapeDtypeStruct((T, D_MODEL), F32),
        compiler_params=pltpu.CompilerParams(
            dimension_semantics=("arbitrary",), vmem_limit_bytes=VMEM_LIMIT_BYTES),
        name="gated_merge",
    )(x2d, attn, y, proj, proj, b_gate, w_attn_o, w_ssd_o, w_out)


def _ffn_kernel(h_ref, n2_ref, wup_ref, cw_ref, cb_ref, wdn_ref, fn_ref, o_ref,
                buf_ref, tail_ref, act_ref):
    first = pl.program_id(1) == 0
    tm = h_ref.shape[0]
    h = h_ref[...]
    ms = jnp.mean(h * h, axis=-1, keepdims=True)
    hn = (h * lax.rsqrt(ms + EPS) * n2_ref[...]).astype(BF16)

    @pl.when(first)
    def _():
        tail_ref[...] = jnp.zeros(tail_ref.shape, F32)

    for c in range(FFN_NCHUNK):
        u = jnp.dot(hn, wup_ref[c], preferred_element_type=F32)
        buf_ref[0:SUBLANES, :] = tail_ref[c]
        buf_ref[SUBLANES:SUBLANES + tm, :] = u
        tail_ref[c] = u[tm - SUBLANES:tm, :]
        acc = u * cw_ref[c, FFN_CONV - 1:FFN_CONV, :] + cb_ref[c]
        for k in range(FFN_CONV - 1):
            off = SUBLANES - (FFN_CONV - 1) + k
            acc = acc + buf_ref[off:off + tm, :] * cw_ref[c, k:k + 1, :]
        val = acc[:, :FFN_CW]
        gt = acc[:, FFN_CW:]
        act_ref[:, c * FFN_CW:(c + 1) * FFN_CW] = (_silu(gt) * val).astype(BF16)

    h2 = h + jnp.dot(act_ref[...], wdn_ref[...], preferred_element_type=F32)
    ms2 = jnp.mean(h2 * h2, axis=-1, keepdims=True)
    o_ref[...] = h2 * lax.rsqrt(ms2 + EPS) * fn_ref[...]


def _ffn(h2d, norm2_w, w_up_r, conv_w_r, conv_b_r, w_down, final_w, batch, seq):
    tm = min(FFN_TM, seq)
    nt = seq // tm
    T = batch * seq
    full = lambda shape: pl.BlockSpec(shape, lambda b, n: (0,) * len(shape))
    return pl.pallas_call(
        _ffn_kernel,
        grid=(batch, nt),
        in_specs=[
            pl.BlockSpec((tm, D_MODEL), lambda b, n: (b * nt + n, 0)),
            full((1, D_MODEL)),
            full((FFN_NCHUNK, D_MODEL, 2 * FFN_CW)),
            full((FFN_NCHUNK, FFN_CONV, 2 * FFN_CW)),
            full((FFN_NCHUNK, 1, 2 * FFN_CW)),
            full((D_FF, D_MODEL)),
            full((1, D_MODEL)),
        ],
        out_specs=pl.BlockSpec((tm, D_MODEL), lambda b, n: (b * nt + n, 0)),
        out_shape=jax.ShapeDtypeStruct((T, D_MODEL), F32),
        scratch_shapes=[
            pltpu.VMEM((SUBLANES + tm, 2 * FFN_CW), F32),
            pltpu.VMEM((FFN_NCHUNK, SUBLANES, 2 * FFN_CW), F32),
            pltpu.VMEM((tm, D_FF), BF16),
        ],
        compiler_params=pltpu.CompilerParams(
            dimension_semantics=("arbitrary", "arbitrary"), vmem_limit_bytes=VMEM_LIMIT_BYTES),
        name="conv_ffn",
    )(h2d, norm2_w, w_up_r, conv_w_r, conv_b_r, w_down, final_w)


def _chunk_val_gate(t):
    lead = t.shape[:-1]
    v = t[..., :D_FF].reshape(lead + (FFN_NCHUNK, FFN_CW))
    g = t[..., D_FF:].reshape(lead + (FFN_NCHUNK, FFN_CW))
    vg = jnp.concatenate([v, g], axis=-1)
    return jnp.moveaxis(vg, -2, 0)


def kernel(x, norm1_w, w_in, b_gate, attn_sinks, w_attn_o, ssd_conv_w, ssd_conv_b, dt_bias, a_log,
           d_skip, ssd_norm_w, w_ssd_o, w_out, norm2_w, w_up, ffn_conv_w, ffn_conv_b, w_down,
           final_norm_w):
    batch, seq, _ = x.shape
    T = batch * seq
    assert norm1_w.shape[0] == 1, "single-layer kernel"
    assert seq % WINDOW == 0 and seq % CHUNK == 0

    w = w_in[0]
    o_q, o_k, o_v, o_z = 0, Q_DIM, Q_DIM + KV_DIM, Q_DIM + 2 * KV_DIM
    o_xs = o_z + D_INNER
    o_b = o_xs + D_INNER
    o_c = o_b + BC_DIM
    o_dt = o_c + BC_DIM
    o_ga = o_dt + N_SSD_HEADS
    o_gs = o_ga + D_MODEL
    cols = lambda a, n: w[:, a:a + n]
    w_main = jnp.concatenate([
        cols(o_z, D_INNER), cols(o_xs, D_INNER), cols(o_q, Q_DIM), cols(o_ga, D_MODEL),
        cols(o_gs, D_MODEL), cols(o_b, BC_DIM), cols(o_c, BC_DIM), cols(o_k, KV_DIM),
        cols(o_v, KV_DIM)], axis=1).astype(BF16)
    w_dt = cols(o_dt, N_SSD_HEADS)
    w_dt_pad = jnp.pad(w_dt, ((0, 0), (0, LANES - N_SSD_HEADS))).astype(BF16)
    w_dtT = w_dt.T.astype(BF16)
    dtb_row = jnp.pad(dt_bias[0].reshape(1, N_SSD_HEADS), ((0, 0), (0, LANES - N_SSD_HEADS)))
    dtb_col = dt_bias[0].reshape(N_SSD_HEADS, 1)

    x2d = x.reshape(T, D_MODEL)
    proj, dt, dtT = _inproj(x2d, norm1_w[0].reshape(1, D_MODEL), w_main, w_dt_pad, w_dtT,
                            dtb_row, dtb_col)
    attn = _attention(proj, attn_sinks[0], batch, seq)
    y = _ssd(proj, dt, dtT, ssd_conv_w[0], ssd_conv_b[0].reshape(1, -1), a_log[0], d_skip[0],
             ssd_norm_w[0].reshape(1, D_INNER), batch, seq)
    h = _merge(x2d, attn, y, proj, b_gate[0].reshape(1, 2 * D_MODEL), w_attn_o[0].astype(BF16),
               w_ssd_o[0].astype(BF16), w_out[0].astype(BF16))
    out = _ffn(h, norm2_w[0].reshape(1, D_MODEL), _chunk_val_gate(w_up[0]).astype(BF16),
               _chunk_val_gate(ffn_conv_w[0]), _chunk_val_gate(ffn_conv_b[0].reshape(1, -1)),
               w_down[0].astype(BF16), final_norm_w.reshape(1, D_MODEL), batch, seq)
    return out.reshape(batch, seq, D_MODEL)
```

```python
import jax
import jax.numpy as jnp
from jax import lax
from jax.experimental import pallas as pl
from jax.experimental.pallas import tpu as pltpu

F32 = jnp.float32
BF16 = jnp.bfloat16

D_MODEL = 1024
N_Q_HEADS = 16
N_KV_HEADS = 4
Q_PER_KV = N_Q_HEADS // N_KV_HEADS
HEAD_DIM = 64
WINDOW = 128
D_INNER = 2048
SSD_HEAD_DIM = 64
N_SSD_HEADS = 32
N_SSD_GROUPS = 4
HEADS_PER_GROUP = N_SSD_HEADS // N_SSD_GROUPS
D_STATE = 128
SSD_CONV = 4
CHUNK = 128
D_FF = 2816
FFN_CONV = 3
EPS = 1e-5
NEG = -1e30
LOG2E = 1.4426950408889634
Q_DIM = N_Q_HEADS * HEAD_DIM
KV_DIM = N_KV_HEADS * HEAD_DIM
BC_DIM = N_SSD_GROUPS * D_STATE
GROUP_W = D_INNER // N_SSD_GROUPS

LANES = 128
SUBLANES = 8
BF16_ROWS = 16
VMEM_LIMIT_BYTES = 56 * 1024 * 1024

OFF_Z = 0
OFF_XS = OFF_Z + D_INNER
OFF_Q = OFF_XS + D_INNER
OFF_GA = OFF_Q + Q_DIM
OFF_GS = OFF_GA + D_MODEL
OFF_B = OFF_GS + D_MODEL
OFF_C = OFF_B + BC_DIM
OFF_K = OFF_C + BC_DIM
OFF_V = OFF_K + KV_DIM
PROJ_W = OFF_V + KV_DIM

IN_TM = 1024
IN_TN = 512
SSD_SUB = 4
MERGE_TM = 512
FFN_TM = 512
FFN_CW = 256
FFN_NCHUNK = D_FF // FFN_CW


def _silu_of_half(h):
    return h + h * jnp.tanh(h)


def _sigmoid(x):
    return 1.0 / (1.0 + jnp.exp(-x))


def _softplus(x):
    return jnp.maximum(x, 0.0) + jnp.log(1.0 + jnp.exp(-jnp.abs(x)))


def _tile3_heads(t):
    pad = jnp.zeros(t.shape[:-1] + (LANES - 3 * N_SSD_HEADS,), t.dtype)
    return jnp.concatenate([t, t, t, pad], axis=-1)


def _split3(x):
    hi = x.astype(BF16)
    r1 = x - hi.astype(F32)
    mid = r1.astype(BF16)
    lo = (r1 - mid.astype(F32)).astype(BF16)
    return hi, mid, lo


def _inproj_kernel(x_ref, nw_ref, w_ref, wdt_ref, wdtT_ref, dtb_ref, dtbT_ref,
                   proj_ref, dt_ref, dtT_ref, xn_ref):
    @pl.when(pl.program_id(1) == 0)
    def _():
        x = x_ref[...]
        ms = jnp.mean(x * x, axis=-1, keepdims=True)
        xn = (x * lax.rsqrt(ms + EPS) * nw_ref[...]).astype(BF16)
        xn_ref[...] = xn
        dt_raw = jnp.dot(xn, wdt_ref[...], preferred_element_type=F32)
        dt_ref[...] = _softplus(dt_raw + dtb_ref[...])
        dtT_raw = lax.dot_general(wdtT_ref[...], xn, (((1,), (1,)), ((), ())),
                                  preferred_element_type=F32)
        dtT = _softplus(dtT_raw + dtbT_ref[...])
        for c in range(dtT_ref.shape[0]):
            dtT_ref[c] = dtT[:, c * CHUNK:(c + 1) * CHUNK]

    proj_ref[...] = jnp.dot(xn_ref[...], w_ref[...], preferred_element_type=F32).astype(BF16)


def _inproj(x2d, norm_w, w_main, w_dt, w_dtT, dt_bias_row, dt_bias_col):
    T = x2d.shape[0]
    tm = min(IN_TM, T)
    grid = (T // tm, PROJ_W // IN_TN)
    return pl.pallas_call(
        _inproj_kernel,
        grid=grid,
        in_specs=[
            pl.BlockSpec((tm, D_MODEL), lambda i, j: (i, 0)),
            pl.BlockSpec((1, D_MODEL), lambda i, j: (0, 0)),
            pl.BlockSpec((D_MODEL, IN_TN), lambda i, j: (0, j)),
            pl.BlockSpec((D_MODEL, LANES), lambda i, j: (0, 0)),
            pl.BlockSpec((N_SSD_HEADS, D_MODEL), lambda i, j: (0, 0)),
            pl.BlockSpec((1, LANES), lambda i, j: (0, 0)),
            pl.BlockSpec((N_SSD_HEADS, 1), lambda i, j: (0, 0)),
        ],
        out_specs=[
            pl.BlockSpec((tm, IN_TN), lambda i, j: (i, j)),
            pl.BlockSpec((tm, LANES), lambda i, j: (i, 0)),
            pl.BlockSpec((tm // CHUNK, N_SSD_HEADS, CHUNK), lambda i, j: (i, 0, 0)),
        ],
        out_shape=[
            jax.ShapeDtypeStruct((T, PROJ_W), BF16),
            jax.ShapeDtypeStruct((T, LANES), F32),
            jax.ShapeDtypeStruct((T // CHUNK, N_SSD_HEADS, CHUNK), F32),
        ],
        scratch_shapes=[pltpu.VMEM((tm, D_MODEL), BF16)],
        compiler_params=pltpu.CompilerParams(
            dimension_semantics=("arbitrary", "arbitrary"), vmem_limit_bytes=VMEM_LIMIT_BYTES),
        name="inproj",
    )(x2d, norm_w, w_main, w_dt, w_dtT, dt_bias_row, dt_bias_col)


def _attn_kernel(sinks_ref, q_ref, kp_ref, kc_ref, vp_ref, vc_ref, o_ref):
    n = pl.program_id(1)
    W = WINDOW
    qi = lax.broadcasted_iota(jnp.int32, (W, 2 * W), 0)
    si = lax.broadcasted_iota(jnp.int32, (W, 2 * W), 1)
    valid = jnp.logical_or(jnp.logical_and(jnp.logical_and(si < W, si > qi), n > 0),
                           jnp.logical_and(si >= W, si - W <= qi))
    q = q_ref[...] * (HEAD_DIM ** -0.5)
    outs = []
    for h in range(N_KV_HEADS):
        cs = slice(h * HEAD_DIM, (h + 1) * HEAD_DIM)
        kh = jnp.concatenate([kp_ref[:, cs], kc_ref[:, cs]], axis=0)
        vh = jnp.concatenate([vp_ref[:, cs], vc_ref[:, cs]], axis=0)
        for g in range(Q_PER_KV):
            head = h * Q_PER_KV + g
            qg = q[:, head * HEAD_DIM:(head + 1) * HEAD_DIM]
            s = lax.dot_general(qg, kh, (((1,), (1,)), ((), ())), preferred_element_type=F32)
            s = jnp.where(valid, s, NEG)
            sink = sinks_ref[head]
            m = jnp.maximum(jnp.max(s, axis=-1, keepdims=True), sink)
            p = jnp.exp(s - m)
            denom = jnp.sum(p, axis=-1, keepdims=True) + jnp.exp(sink - m)
            o = jnp.dot(p.astype(BF16), vh, preferred_element_type=F32)
            outs.append(o * (1.0 / denom))
    o_ref[...] = jnp.concatenate(outs, axis=-1).astype(BF16)


def _attention(proj, sinks, batch, seq):
    nb = seq // WINDOW
    T = batch * seq
    row = lambda b, n: b * nb + n
    prow = lambda b, n: b * nb + jnp.maximum(n - 1, 0)
    return pl.pallas_call(
        _attn_kernel,
        grid=(batch, nb),
        in_specs=[
            pl.BlockSpec(memory_space=pltpu.SMEM),
            pl.BlockSpec((WINDOW, Q_DIM), lambda b, n: (row(b, n), OFF_Q // Q_DIM)),
            pl.BlockSpec((WINDOW, KV_DIM), lambda b, n: (prow(b, n), OFF_K // KV_DIM)),
            pl.BlockSpec((WINDOW, KV_DIM), lambda b, n: (row(b, n), OFF_K // KV_DIM)),
            pl.BlockSpec((WINDOW, KV_DIM), lambda b, n: (prow(b, n), OFF_V // KV_DIM)),
            pl.BlockSpec((WINDOW, KV_DIM), lambda b, n: (row(b, n), OFF_V // KV_DIM)),
        ],
        out_specs=pl.BlockSpec((WINDOW, Q_DIM), lambda b, n: (row(b, n), 0)),
        out_shape=jax.ShapeDtypeStruct((T, Q_DIM), BF16),
        compiler_params=pltpu.CompilerParams(
            dimension_semantics=("arbitrary", "arbitrary"), vmem_limit_bytes=VMEM_LIMIT_BYTES),
        name="swa_attention",
    )(sinks, proj, proj, proj, proj, proj)


def _shift_matrix(taps, L, tail):
    r = jnp.arange((taps - 1) * L)[:, None]
    c = jnp.arange(tail + L)[None, :]
    return (c == (r % L) + tail - (taps - 1 - r // L)).astype(BF16)


def _head_expand_matrix():
    k = jnp.arange(LANES)[:, None]
    c = jnp.arange(D_INNER)[None, :]
    return jnp.logical_and(k < 3 * N_SSD_HEADS, k % N_SSD_HEADS == c // SSD_HEAD_DIM).astype(BF16)


def _split3_lanes(v, lane):
    hi = v.astype(BF16).astype(F32)
    r1 = v - hi
    mid = r1.astype(BF16).astype(F32)
    parts = jnp.where(lane < N_SSD_HEADS, hi, jnp.where(lane < 2 * N_SSD_HEADS, mid, r1 - mid))
    return parts.astype(BF16)


def _causal_conv(carry_ref, raw_ref, i, cs, smat, w_ref, b_ref, taps):
    L = smat.shape[0] // (taps - 1)
    r0 = pl.multiple_of(i * L, L)
    raw = raw_ref[pl.ds(r0, L), cs]
    before = pl.multiple_of(jnp.maximum(r0 - BF16_ROWS, 0), BF16_ROWS)
    tail = jnp.where(i > 0, raw_ref[pl.ds(before, BF16_ROWS), cs], carry_ref[:, cs])
    ext = jnp.concatenate([tail, raw], axis=0)
    sh = jnp.dot(smat, ext, preferred_element_type=F32)
    acc = raw.astype(F32) * w_ref[taps - 1:taps, cs] + b_ref[:, cs]
    for k in range(taps - 1):
        acc = acc + sh[k * L:(k + 1) * L, :] * w_ref[k:k + 1, cs]
    return acc


def _ssd_kernel(z_ref, xs_ref, b_ref, c_ref, dt_ref, dtT_ref,
                cwx_ref, cbx_ref, cwb_ref, cbb_ref, cwc_ref, cbc_ref,
                alog_ref, alogT_ref, dskip_ref, nw_ref, smat_ref, emat_ref,
                y_ref,
                state_ref, tailx_ref, tailb_ref, tailc_ref):
    L = CHUNK

    @pl.when(pl.program_id(1) == 0)
    def _():
        state_ref[...] = jnp.zeros(state_ref.shape, F32)
        tailx_ref[...] = jnp.zeros(tailx_ref.shape, BF16)
        tailb_ref[...] = jnp.zeros(tailb_ref.shape, BF16)
        tailc_ref[...] = jnp.zeros(tailc_ref.shape, BF16)

    def chunk(i, carry):
        r0 = pl.multiple_of(i * L, L)
        rows = pl.ds(r0, L)
        smat = smat_ref[...]
        allc = slice(0, BC_DIM)
        bm = _silu_of_half(_causal_conv(tailb_ref, b_ref, i, allc, smat, cwb_ref, cbb_ref, SSD_CONV))
        cm = _silu_of_half(_causal_conv(tailc_ref, c_ref, i, allc, smat, cwc_ref, cbc_ref, SSD_CONV))

        dt = dt_ref[rows, :]
        dtT = dtT_ref[i]
        dA = dt * (-jnp.exp(alog_ref[...]))
        dAT = dtT * (-jnp.exp(alogT_ref[...]))

        ri = lax.broadcasted_iota(jnp.int32, (L, L), 0)
        ci = lax.broadcasted_iota(jnp.int32, (L, L), 1)
        causal = ci <= ri
        tri = jnp.where(causal, 1.0, 0.0).astype(BF16)
        triT = jnp.where(ri <= ci, 1.0, 0.0).astype(BF16)
        a_cs = sum(jnp.dot(tri, p, preferred_element_type=F32) for p in _split3(dA))
        a_csT = sum(jnp.dot(p, triT, preferred_element_type=F32) for p in _split3(dAT))

        a_last = a_cs[L - 1:L, :]
        ea = jnp.exp(a_cs)
        w_state = dt * jnp.exp(a_last - a_cs)
        a2 = a_cs * LOG2E
        a2T = (a_csT - jnp.log(dtT)) * LOG2E

        lane = lax.broadcasted_iota(jnp.int32, (L, LANES), 1)
        lane_lt_half = lane < SSD_HEAD_DIM
        ea_parts = _split3_lanes(ea, lane)
        ws_parts = _split3_lanes(w_state, lane)

        for g in range(N_SSD_GROUPS):
            gcol = slice(g * GROUP_W, (g + 1) * GROUP_W)
            xs = _silu_of_half(_causal_conv(tailx_ref, xs_ref, i, gcol, smat, cwx_ref, cbx_ref, SSD_CONV))
            xs_bf = xs.astype(BF16)
            bg = bm[:, g * D_STATE:(g + 1) * D_STATE]
            bg_bf = bg.astype(BF16)
            cg_bf = cm[:, g * D_STATE:(g + 1) * D_STATE].astype(BF16)
            cb = lax.dot_general(cg_bf, bg_bf, (((1,), (1,)), ((), ())), preferred_element_type=F32)
            ea_g = jnp.dot(ea_parts, emat_ref[:, gcol], preferred_element_type=F32)
            ws_g = jnp.dot(ws_parts, emat_ref[:, gcol], preferred_element_type=F32)
            prev = state_ref[g]
            y_off = jnp.dot(cg_bf, prev.astype(BF16), preferred_element_type=F32) * ea_g
            xw = (xs * ws_g).astype(BF16)
            new_states = jnp.dot(bg.T.astype(BF16), xw, preferred_element_type=F32)
            state_ref[g] = prev * ea_g[L - 1:L, :] + new_states
            y_pairs = []
            for jp in range(HEADS_PER_GROUP // 2):
                h0 = g * HEADS_PER_GROUP + 2 * jp
                xs_pair_bf = xs_bf[:, jp * LANES:(jp + 1) * LANES]
                yd = []
                for h in (h0, h0 + 1):
                    seg2 = a2[:, h:h + 1] - a2T[h:h + 1, :]
                    m = cb * jnp.exp2(jnp.where(causal, seg2, NEG))
                    yd.append(jnp.dot(m.astype(BF16), xs_pair_bf, preferred_element_type=F32))
                y_pairs.append(jnp.where(lane_lt_half, yd[0], yd[1])
                               + y_off[:, jp * LANES:(jp + 1) * LANES])
            y = jnp.concatenate(y_pairs, axis=-1) + xs * dskip_ref[:, gcol]
            y = y * _silu_of_half(z_ref[rows, gcol].astype(F32))
            ms = jnp.mean(y * y, axis=-1, keepdims=True)
            y_ref[rows, gcol] = (y * lax.rsqrt(ms + EPS) * nw_ref[:, gcol]).astype(BF16)

        return carry

    lax.fori_loop(0, z_ref.shape[0] // L, chunk, 0)
    last = slice(z_ref.shape[0] - BF16_ROWS, z_ref.shape[0])
    tailx_ref[...] = xs_ref[last, :]
    tailb_ref[...] = b_ref[last, :]
    tailc_ref[...] = c_ref[last, :]


def _ssd(proj, dt, dtT, conv_w, conv_b, a_log, d_skip, norm_w, batch, seq):
    sub = min(SSD_SUB, seq // CHUNK)
    rows = sub * CHUNK
    nc = seq // rows
    T = batch * seq
    row = lambda b, n: b * nc + n
    full = lambda shape: pl.BlockSpec(shape, lambda b, n: (0,) * len(shape))
    conv_w = 0.5 * conv_w
    conv_b = 0.5 * conv_b
    cwx, cwb, cwc = conv_w[:, :D_INNER], conv_w[:, D_INNER:D_INNER + BC_DIM], conv_w[:, D_INNER + BC_DIM:]
    cbx, cbb, cbc = conv_b[:, :D_INNER], conv_b[:, D_INNER:D_INNER + BC_DIM], conv_b[:, D_INNER + BC_DIM:]
    alog_row = _tile3_heads(a_log.reshape(1, N_SSD_HEADS))
    alog_col = a_log.reshape(N_SSD_HEADS, 1)
    dskip_row = jnp.repeat(d_skip, SSD_HEAD_DIM).reshape(1, D_INNER)
    smat = _shift_matrix(SSD_CONV, CHUNK, BF16_ROWS)
    emat = _head_expand_matrix()
    return pl.pallas_call(
        _ssd_kernel,
        grid=(batch, nc),
        in_specs=[
            pl.BlockSpec((rows, D_INNER), lambda b, n: (row(b, n), OFF_Z // D_INNER)),
            pl.BlockSpec((rows, D_INNER), lambda b, n: (row(b, n), OFF_XS // D_INNER)),
            pl.BlockSpec((rows, BC_DIM), lambda b, n: (row(b, n), OFF_B // BC_DIM)),
            pl.BlockSpec((rows, BC_DIM), lambda b, n: (row(b, n), OFF_C // BC_DIM)),
            pl.BlockSpec((rows, LANES), lambda b, n: (row(b, n), 0)),
            pl.BlockSpec((sub, N_SSD_HEADS, CHUNK), lambda b, n: (row(b, n), 0, 0)),
            full((SSD_CONV, D_INNER)), full((1, D_INNER)),
            full((SSD_CONV, BC_DIM)), full((1, BC_DIM)),
            full((SSD_CONV, BC_DIM)), full((1, BC_DIM)),
            full((1, LANES)), full((N_SSD_HEADS, 1)),
            full((1, D_INNER)), full((1, D_INNER)),
            full(smat.shape), full(emat.shape),
        ],
        out_specs=pl.BlockSpec((rows, D_INNER), lambda b, n: (row(b, n), 0)),
        out_shape=jax.ShapeDtypeStruct((T, D_INNER), BF16),
        scratch_shapes=[
            pltpu.VMEM((N_SSD_GROUPS, D_STATE, GROUP_W), F32),
            pltpu.VMEM((BF16_ROWS, D_INNER), BF16),
            pltpu.VMEM((BF16_ROWS, BC_DIM), BF16),
            pltpu.VMEM((BF16_ROWS, BC_DIM), BF16),
        ],
        compiler_params=pltpu.CompilerParams(
            dimension_semantics=("arbitrary", "arbitrary"), vmem_limit_bytes=VMEM_LIMIT_BYTES),
        name="ssd_mixer",
    )(proj, proj, proj, proj, dt, dtT, cwx, cbx, cwb, cbb, cwc, cbc,
      alog_row, alog_col, dskip_row, norm_w, smat, emat)


def _merge_kernel(x_ref, attn_ref, y_ref, ga_ref, gs_ref, bg_ref, wa_ref, ws_ref, wo_ref, h_ref):
    attn = jnp.dot(attn_ref[...], wa_ref[...], preferred_element_type=F32)
    ssd = jnp.dot(y_ref[...], ws_ref[...], preferred_element_type=F32)
    gate_a = _sigmoid(ga_ref[...].astype(F32) + bg_ref[:, :D_MODEL])
    gate_s = _sigmoid(gs_ref[...].astype(F32) + bg_ref[:, D_MODEL:])
    mixed = (gate_a * attn + gate_s * ssd).astype(BF16)
    h_ref[...] = x_ref[...] + jnp.dot(mixed, wo_ref[...], preferred_element_type=F32)


def _merge(x2d, attn, y, proj, b_gate, w_attn_o, w_ssd_o, w_out):
    T = x2d.shape[0]
    tm = min(MERGE_TM, T)
    full = lambda shape: pl.BlockSpec(shape, lambda i: (0,) * len(shape))
    return pl.pallas_call(
        _merge_kernel,
        grid=(T // tm,),
        in_specs=[
            pl.BlockSpec((tm, D_MODEL), lambda i: (i, 0)),
            pl.BlockSpec((tm, Q_DIM), lambda i: (i, 0)),
            pl.BlockSpec((tm, D_INNER), lambda i: (i, 0)),
            pl.BlockSpec((tm, D_MODEL), lambda i: (i, OFF_GA // D_MODEL)),
            pl.BlockSpec((tm, D_MODEL), lambda i: (i, OFF_GS // D_MODEL)),
            full((1, 2 * D_MODEL)),
            full((Q_DIM, D_MODEL)), full((D_INNER, D_MODEL)), full((D_MODEL, D_MODEL)),
        ],
        out_specs=pl.BlockSpec((tm, D_MODEL), lambda i: (i, 0)),
        out_shape=jax.ShapeDtypeStruct((T, D_MODEL), F32),
        compiler_params=pltpu.CompilerParams(
            dimension_semantics=("arbitrary",), vmem_limit_bytes=VMEM_LIMIT_BYTES),
        name="gated_merge",
    )(x2d, attn, y, proj, proj, b_gate, w_attn_o, w_ssd_o, w_out)


def _ffn_kernel(h_ref, n2_ref, wup_ref, cw_ref, cb_ref, wdn_ref, fn_ref, o_ref,
                buf_ref, tail_ref, act_ref):
    first = pl.program_id(1) == 0
    tm = h_ref.shape[0]
    h = h_ref[...]
    ms = jnp.mean(h * h, axis=-1, keepdims=True)
    hn = (h * lax.rsqrt(ms + EPS) * n2_ref[...]).astype(BF16)

    @pl.when(first)
    def _():
        tail_ref[...] = jnp.zeros(tail_ref.shape, F32)

    for c in range(FFN_NCHUNK):
        u = jnp.dot(hn, wup_ref[c], preferred_element_type=F32)
        buf_ref[0:SUBLANES, :] = tail_ref[c]
        buf_ref[SUBLANES:SUBLANES + tm, :] = u
        tail_ref[c] = u[tm - SUBLANES:tm, :]
        acc = u * cw_ref[c, FFN_CONV - 1:FFN_CONV, :] + cb_ref[c]
        for k in range(FFN_CONV - 1):
            off = SUBLANES - (FFN_CONV - 1) + k
            acc = acc + buf_ref[off:off + tm, :] * cw_ref[c, k:k + 1, :]
        val = acc[:, :FFN_CW]
        half_gate = acc[:, FFN_CW:]
        act_ref[:, c * FFN_CW:(c + 1) * FFN_CW] = (_silu_of_half(half_gate) * val).astype(BF16)

    h2 = h + jnp.dot(act_ref[...], wdn_ref[...], preferred_element_type=F32)
    ms2 = jnp.mean(h2 * h2, axis=-1, keepdims=True)
    o_ref[...] = h2 * lax.rsqrt(ms2 + EPS) * fn_ref[...]


def _ffn(h2d, norm2_w, w_up_r, conv_w_r, conv_b_r, w_down, final_w, batch, seq):
    tm = min(FFN_TM, seq)
    nt = seq // tm
    T = batch * seq
    full = lambda shape: pl.BlockSpec(shape, lambda b, n: (0,) * len(shape))
    return pl.pallas_call(
        _ffn_kernel,
        grid=(batch, nt),
        in_specs=[
            pl.BlockSpec((tm, D_MODEL), lambda b, n: (b * nt + n, 0)),
            full((1, D_MODEL)),
            full((FFN_NCHUNK, D_MODEL, 2 * FFN_CW)),
            full((FFN_NCHUNK, FFN_CONV, 2 * FFN_CW)),
            full((FFN_NCHUNK, 1, 2 * FFN_CW)),
            full((D_FF, D_MODEL)),
            full((1, D_MODEL)),
        ],
        out_specs=pl.BlockSpec((tm, D_MODEL), lambda b, n: (b * nt + n, 0)),
        out_shape=jax.ShapeDtypeStruct((T, D_MODEL), F32),
        scratch_shapes=[
            pltpu.VMEM((SUBLANES + tm, 2 * FFN_CW), F32),
            pltpu.VMEM((FFN_NCHUNK, SUBLANES, 2 * FFN_CW), F32),
            pltpu.VMEM((tm, D_FF), BF16),
        ],
        compiler_params=pltpu.CompilerParams(
            dimension_semantics=("arbitrary", "arbitrary"), vmem_limit_bytes=VMEM_LIMIT_BYTES),
        name="conv_ffn",
    )(h2d, norm2_w, w_up_r, conv_w_r, conv_b_r, w_down, final_w)


def _chunk_val_gate(t, gate_scale=1.0):
    lead = t.shape[:-1]
    v = t[..., :D_FF].reshape(lead + (FFN_NCHUNK, FFN_CW))
    g = (gate_scale * t[..., D_FF:]).reshape(lead + (FFN_NCHUNK, FFN_CW))
    vg = jnp.concatenate([v, g], axis=-1)
    return jnp.moveaxis(vg, -2, 0)


def kernel(x, norm1_w, w_in, b_gate, attn_sinks, w_attn_o, ssd_conv_w, ssd_conv_b, dt_bias, a_log,
           d_skip, ssd_norm_w, w_ssd_o, w_out, norm2_w, w_up, ffn_conv_w, ffn_conv_b, w_down,
           final_norm_w):
    batch, seq, _ = x.shape
    T = batch * seq
    assert norm1_w.shape[0] == 1, "single-layer kernel"
    assert seq % WINDOW == 0 and seq % CHUNK == 0

    w = w_in[0]
    o_q, o_k, o_v, o_z = 0, Q_DIM, Q_DIM + KV_DIM, Q_DIM + 2 * KV_DIM
    o_xs = o_z + D_INNER
    o_b = o_xs + D_INNER
    o_c = o_b + BC_DIM
    o_dt = o_c + BC_DIM
    o_ga = o_dt + N_SSD_HEADS
    o_gs = o_ga + D_MODEL
    cols = lambda a, n: w[:, a:a + n]
    w_main = jnp.concatenate([
        0.5 * cols(o_z, D_INNER), cols(o_xs, D_INNER), cols(o_q, Q_DIM), cols(o_ga, D_MODEL),
        cols(o_gs, D_MODEL), cols(o_b, BC_DIM), cols(o_c, BC_DIM), cols(o_k, KV_DIM),
        cols(o_v, KV_DIM)], axis=1).astype(BF16)
    w_dt = cols(o_dt, N_SSD_HEADS)
    w_dt_pad = _tile3_heads(w_dt).astype(BF16)
    w_dtT = w_dt.T.astype(BF16)
    dtb_row = _tile3_heads(dt_bias[0].reshape(1, N_SSD_HEADS))
    dtb_col = dt_bias[0].reshape(N_SSD_HEADS, 1)

    x2d = x.reshape(T, D_MODEL)
    proj, dt, dtT = _inproj(x2d, norm1_w[0].reshape(1, D_MODEL), w_main, w_dt_pad, w_dtT,
                            dtb_row, dtb_col)
    attn = _attention(proj, attn_sinks[0], batch, seq)
    y = _ssd(proj, dt, dtT, ssd_conv_w[0], ssd_conv_b[0].reshape(1, -1), a_log[0], d_skip[0],
             ssd_norm_w[0].reshape(1, D_INNER), batch, seq)
    h = _merge(x2d, attn, y, proj, b_gate[0].reshape(1, 2 * D_MODEL), w_attn_o[0].astype(BF16),
               w_ssd_o[0].astype(BF16), w_out[0].astype(BF16))
    out = _ffn(h, norm2_w[0].reshape(1, D_MODEL), _chunk_val_gate(w_up[0]).astype(BF16),
               _chunk_val_gate(ffn_conv_w[0], 0.5), _chunk_val_gate(ffn_conv_b[0].reshape(1, -1), 0.5),
               w_down[0].astype(BF16), final_norm_w.reshape(1, D_MODEL), batch, seq)
    return out.reshape(batch, seq, D_MODEL)
```

```python
import jax
import jax.numpy as jnp
from jax import lax
from jax.experimental import pallas as pl
from jax.experimental.pallas import tpu as pltpu

F32 = jnp.float32
BF16 = jnp.bfloat16

D_MODEL = 1024
N_Q_HEADS = 16
N_KV_HEADS = 4
Q_PER_KV = N_Q_HEADS // N_KV_HEADS
HEAD_DIM = 64
WINDOW = 128
D_INNER = 2048
SSD_HEAD_DIM = 64
N_SSD_HEADS = 32
N_SSD_GROUPS = 4
HEADS_PER_GROUP = N_SSD_HEADS // N_SSD_GROUPS
D_STATE = 128
SSD_CONV = 4
CHUNK = 128
D_FF = 2816
FFN_CONV = 3
EPS = 1e-5
NEG = -1e30
LOG2E = 1.4426950408889634
Q_DIM = N_Q_HEADS * HEAD_DIM
KV_DIM = N_KV_HEADS * HEAD_DIM
BC_DIM = N_SSD_GROUPS * D_STATE
GROUP_W = D_INNER // N_SSD_GROUPS

LANES = 128
SUBLANES = 8
BF16_ROWS = 16
VMEM_LIMIT_BYTES = 56 * 1024 * 1024

OFF_Z = 0
OFF_XS = OFF_Z + D_INNER
OFF_Q = OFF_XS + D_INNER
OFF_GA = OFF_Q + Q_DIM
OFF_GS = OFF_GA + D_MODEL
OFF_B = OFF_GS + D_MODEL
OFF_C = OFF_B + BC_DIM
OFF_K = OFF_C + BC_DIM
OFF_V = OFF_K + KV_DIM
PROJ_W = OFF_V + KV_DIM

IN_TM = 512
IN_TN = 512
ATTN_ROWS = 512
SSD_SUB = 4
MERGE_TM = 512
FFN_TM = 512
FFN_CW = 256
FFN_NCHUNK = D_FF // FFN_CW


def _silu_of_half(h):
    return h + h * jnp.tanh(h)


def _sigmoid(x):
    return 1.0 / (1.0 + jnp.exp(-x))


def _softplus(x):
    return jnp.maximum(x, 0.0) + jnp.log(1.0 + jnp.exp(-jnp.abs(x)))


def _tile3_heads(t):
    pad = jnp.zeros(t.shape[:-1] + (LANES - 3 * N_SSD_HEADS,), t.dtype)
    return jnp.concatenate([t, t, t, pad], axis=-1)


def _split3(x):
    hi = x.astype(BF16)
    r1 = x - hi.astype(F32)
    mid = r1.astype(BF16)
    lo = (r1 - mid.astype(F32)).astype(BF16)
    return hi, mid, lo


def _inproj_kernel(x_ref, nw_ref, w_ref, wdt_ref, wdtT_ref, dtb_ref, dtbT_ref,
                   proj_ref, dt_ref, dtT_ref):
    x = x_ref[...]
    ms = jnp.mean(x * x, axis=-1, keepdims=True)
    xn = (x * lax.rsqrt(ms + EPS) * nw_ref[...]).astype(BF16)
    for c in range(PROJ_W // IN_TN):
        cs = slice(c * IN_TN, (c + 1) * IN_TN)
        proj_ref[:, cs] = jnp.dot(xn, w_ref[:, cs], preferred_element_type=F32).astype(BF16)
    dt_raw = jnp.dot(xn, wdt_ref[...], preferred_element_type=F32)
    dt_ref[...] = _softplus(dt_raw + dtb_ref[...])
    dtT_raw = lax.dot_general(wdtT_ref[...], xn, (((1,), (1,)), ((), ())),
                              preferred_element_type=F32)
    dtT = _softplus(dtT_raw + dtbT_ref[...])
    for c in range(dtT_ref.shape[0]):
        dtT_ref[c] = dtT[:, c * CHUNK:(c + 1) * CHUNK]


def _resident(shape):
    return pl.BlockSpec(shape, lambda *_: (0,) * len(shape), pipeline_mode=pl.Buffered(1))


def _inproj(x2d, norm_w, w_main, w_dt, w_dtT, dt_bias_row, dt_bias_col):
    T = x2d.shape[0]
    tm = min(IN_TM, T)
    return pl.pallas_call(
        _inproj_kernel,
        grid=(T // tm,),
        in_specs=[
            pl.BlockSpec((tm, D_MODEL), lambda i: (i, 0)),
            _resident((1, D_MODEL)),
            _resident((D_MODEL, PROJ_W)),
            _resident((D_MODEL, LANES)),
            _resident((N_SSD_HEADS, D_MODEL)),
            _resident((1, LANES)),
            _resident((N_SSD_HEADS, 1)),
        ],
        out_specs=[
            pl.BlockSpec((tm, PROJ_W), lambda i: (i, 0)),
            pl.BlockSpec((tm, LANES), lambda i: (i, 0)),
            pl.BlockSpec((tm // CHUNK, N_SSD_HEADS, CHUNK), lambda i: (i, 0, 0)),
        ],
        out_shape=[
            jax.ShapeDtypeStruct((T, PROJ_W), BF16),
            jax.ShapeDtypeStruct((T, LANES), F32),
            jax.ShapeDtypeStruct((T // CHUNK, N_SSD_HEADS, CHUNK), F32),
        ],
        compiler_params=pltpu.CompilerParams(
            dimension_semantics=("arbitrary",), vmem_limit_bytes=VMEM_LIMIT_BYTES),
        name="inproj",
    )(x2d, norm_w, w_main, w_dt, w_dtT, dt_bias_row, dt_bias_col)


ATTN_HEAD_ORDER = tuple(
    (2 * (j // Q_PER_KV) + half) * Q_PER_KV + j % Q_PER_KV
    for j in range(N_Q_HEADS // 2) for half in range(2))


def _attn_kernel(sinks_ref, q_ref, kp_ref, kc_ref, vp_ref, vc_ref, o_ref):
    W = WINDOW
    nsub = q_ref.shape[0] // W
    qi = lax.broadcasted_iota(jnp.int32, (W, 2 * W), 0)
    si = lax.broadcasted_iota(jnp.int32, (W, 2 * W), 1)
    in_prev = jnp.logical_and(si < W, si > qi)
    in_cur = jnp.logical_and(si >= W, si - W <= qi)
    lane = lax.broadcasted_iota(jnp.int32, (W, LANES), 1)
    left = lane < HEAD_DIM

    def block(i, carry):
        r0 = pl.multiple_of(i * W, W)
        rows = pl.ds(r0, W)
        before = pl.ds(pl.multiple_of(jnp.maximum(r0 - W, 0), W), W)
        has_prev = jnp.logical_or(pl.program_id(1) > 0, i > 0)
        valid = jnp.logical_or(jnp.logical_and(in_prev, has_prev), in_cur)
        for kv in range(N_KV_HEADS // 2):
            kvc = slice(kv * LANES, (kv + 1) * LANES)
            k_prev = jnp.where(i > 0, kc_ref[before, kvc], kp_ref[:, kvc])
            v_prev = jnp.where(i > 0, vc_ref[before, kvc], vp_ref[:, kvc])
            kpair = jnp.concatenate([k_prev, kc_ref[rows, kvc]], axis=0)
            vpair = jnp.concatenate([v_prev, vc_ref[rows, kvc]], axis=0)
            for g in range(Q_PER_KV):
                j = kv * Q_PER_KV + g
                col = slice(j * LANES, (j + 1) * LANES)
                qc = q_ref[rows, col].astype(F32) * (HEAD_DIM ** -0.5)
                q2 = jnp.concatenate([jnp.where(left, qc, 0.0), jnp.where(left, 0.0, qc)],
                                     axis=0).astype(BF16)
                s2 = lax.dot_general(q2, kpair, (((1,), (1,)), ((), ())),
                                     preferred_element_type=F32)
                ps, rs = [], []
                for half in range(2):
                    s = jnp.where(valid, s2[half * W:(half + 1) * W, :], NEG)
                    sink = sinks_ref[2 * j + half]
                    m = jnp.maximum(jnp.max(s, axis=-1, keepdims=True), sink)
                    p = jnp.exp(s - m)
                    denom = jnp.sum(p, axis=-1, keepdims=True) + jnp.exp(sink - m)
                    ps.append(p.astype(BF16))
                    rs.append(1.0 / denom)
                o2 = jnp.dot(jnp.concatenate(ps, axis=0), vpair, preferred_element_type=F32)
                o = jnp.where(left, o2[:W, :] * rs[0], o2[W:, :] * rs[1])
                o_ref[rows, col] = o.astype(BF16)
        return carry

    lax.fori_loop(0, nsub, block, 0)


def _attention(proj, sinks_ordered, batch, seq):
    rows = min(ATTN_ROWS, seq)
    sub = rows // WINDOW
    nt = seq // rows
    T = batch * seq
    row = lambda b, n: b * nt + n
    prow = lambda b, n: (b * nt + n) * sub - jnp.minimum(n, 1)
    return pl.pallas_call(
        _attn_kernel,
        grid=(batch, nt),
        in_specs=[
            pl.BlockSpec(memory_space=pltpu.SMEM),
            pl.BlockSpec((rows, Q_DIM), lambda b, n: (row(b, n), OFF_Q // Q_DIM)),
            pl.BlockSpec((WINDOW, KV_DIM), lambda b, n: (prow(b, n), OFF_K // KV_DIM)),
            pl.BlockSpec((rows, KV_DIM), lambda b, n: (row(b, n), OFF_K // KV_DIM)),
            pl.BlockSpec((WINDOW, KV_DIM), lambda b, n: (prow(b, n), OFF_V // KV_DIM)),
            pl.BlockSpec((rows, KV_DIM), lambda b, n: (row(b, n), OFF_V // KV_DIM)),
        ],
        out_specs=pl.BlockSpec((rows, Q_DIM), lambda b, n: (row(b, n), 0)),
        out_shape=jax.ShapeDtypeStruct((T, Q_DIM), BF16),
        compiler_params=pltpu.CompilerParams(
            dimension_semantics=("arbitrary", "arbitrary"), vmem_limit_bytes=VMEM_LIMIT_BYTES),
        name="swa_attention",
    )(sinks_ordered, proj, proj, proj, proj, proj)


def _shift_matrix(taps, L, tail):
    r = jnp.arange((taps - 1) * L)[:, None]
    c = jnp.arange(tail + L)[None, :]
    return (c == (r % L) + tail - (taps - 1 - r // L)).astype(BF16)


def _head_expand_matrix():
    k = jnp.arange(LANES)[:, None]
    c = jnp.arange(D_INNER)[None, :]
    return jnp.logical_and(k < 3 * N_SSD_HEADS, k % N_SSD_HEADS == c // SSD_HEAD_DIM).astype(BF16)


def _split3_lanes(v, lane):
    hi = v.astype(BF16).astype(F32)
    r1 = v - hi
    mid = r1.astype(BF16).astype(F32)
    parts = jnp.where(lane < N_SSD_HEADS, hi, jnp.where(lane < 2 * N_SSD_HEADS, mid, r1 - mid))
    return parts.astype(BF16)


def _causal_conv(carry_ref, raw_ref, i, cs, smat, w_ref, b_ref, taps):
    L = smat.shape[0] // (taps - 1)
    r0 = pl.multiple_of(i * L, L)
    raw = raw_ref[pl.ds(r0, L), cs]
    before = pl.multiple_of(jnp.maximum(r0 - BF16_ROWS, 0), BF16_ROWS)
    tail = jnp.where(i > 0, raw_ref[pl.ds(before, BF16_ROWS), cs], carry_ref[:, cs])
    ext = jnp.concatenate([tail, raw], axis=0)
    sh = jnp.dot(smat, ext, preferred_element_type=F32)
    acc = raw.astype(F32) * w_ref[taps - 1:taps, cs] + b_ref[:, cs]
    for k in range(taps - 1):
        acc = acc + sh[k * L:(k + 1) * L, :] * w_ref[k:k + 1, cs]
    return acc


def _ssd_kernel(z_ref, xs_ref, b_ref, c_ref, dt_ref, dtT_ref,
                cwx_ref, cbx_ref, cwb_ref, cbb_ref, cwc_ref, cbc_ref,
                alog_ref, alogT_ref, dskip_ref, nw_ref, smat_ref, emat_ref,
                y_ref,
                state_ref, tailx_ref, tailb_ref, tailc_ref):
    L = CHUNK

    @pl.when(pl.program_id(1) == 0)
    def _():
        state_ref[...] = jnp.zeros(state_ref.shape, F32)
        tailx_ref[...] = jnp.zeros(tailx_ref.shape, BF16)
        tailb_ref[...] = jnp.zeros(tailb_ref.shape, BF16)
        tailc_ref[...] = jnp.zeros(tailc_ref.shape, BF16)

    def chunk(i, carry):
        r0 = pl.multiple_of(i * L, L)
        rows = pl.ds(r0, L)
        smat = smat_ref[...]
        allc = slice(0, BC_DIM)
        bm = _silu_of_half(_causal_conv(tailb_ref, b_ref, i, allc, smat, cwb_ref, cbb_ref, SSD_CONV))
        cm = _silu_of_half(_causal_conv(tailc_ref, c_ref, i, allc, smat, cwc_ref, cbc_ref, SSD_CONV))

        dt = dt_ref[rows, :]
        dtT = dtT_ref[i]
        dA = dt * (-jnp.exp(alog_ref[...]))
        dAT = dtT * (-jnp.exp(alogT_ref[...]))

        ri = lax.broadcasted_iota(jnp.int32, (L, L), 0)
        ci = lax.broadcasted_iota(jnp.int32, (L, L), 1)
        causal = ci <= ri
        tri = jnp.where(causal, 1.0, 0.0).astype(BF16)
        triT = jnp.where(ri <= ci, 1.0, 0.0).astype(BF16)
        a_cs = sum(jnp.dot(tri, p, preferred_element_type=F32) for p in _split3(dA))
        a_csT = sum(jnp.dot(p, triT, preferred_element_type=F32) for p in _split3(dAT))

        a_last = a_cs[L - 1:L, :]
        ea = jnp.exp(a_cs)
        w_state = dt * jnp.exp(a_last - a_cs)
        a2 = a_cs * LOG2E
        a2T = (a_csT - jnp.log(dtT)) * LOG2E

        lane = lax.broadcasted_iota(jnp.int32, (L, LANES), 1)
        lane_lt_half = lane < SSD_HEAD_DIM
        ea_parts = _split3_lanes(ea, lane)
        ws_parts = _split3_lanes(w_state, lane)

        for g in range(N_SSD_GROUPS):
            gcol = slice(g * GROUP_W, (g + 1) * GROUP_W)
            xs = _silu_of_half(_causal_conv(tailx_ref, xs_ref, i, gcol, smat, cwx_ref, cbx_ref, SSD_CONV))
            xs_bf = xs.astype(BF16)
            bg = bm[:, g * D_STATE:(g + 1) * D_STATE]
            bg_bf = bg.astype(BF16)
            cg_bf = cm[:, g * D_STATE:(g + 1) * D_STATE].astype(BF16)
            cb = lax.dot_general(cg_bf, bg_bf, (((1,), (1,)), ((), ())), preferred_element_type=F32)
            ea_g = jnp.dot(ea_parts, emat_ref[:, gcol], preferred_element_type=F32)
            ws_g = jnp.dot(ws_parts, emat_ref[:, gcol], preferred_element_type=F32)
            prev = state_ref[g]
            y_off = jnp.dot(cg_bf, prev.astype(BF16), preferred_element_type=F32) * ea_g
            xw = (xs * ws_g).astype(BF16)
            new_states = jnp.dot(bg.T.astype(BF16), xw, preferred_element_type=F32)
            state_ref[g] = prev * ea_g[L - 1:L, :] + new_states
            y_pairs = []
            for jp in range(HEADS_PER_GROUP // 2):
                h0 = g * HEADS_PER_GROUP + 2 * jp
                xs_pair_bf = xs_bf[:, jp * LANES:(jp + 1) * LANES]
                yd = []
                for h in (h0, h0 + 1):
                    seg2 = a2[:, h:h + 1] - a2T[h:h + 1, :]
                    m = cb * jnp.exp2(jnp.where(causal, seg2, NEG))
                    yd.append(jnp.dot(m.astype(BF16), xs_pair_bf, preferred_element_type=F32))
                y_pairs.append(jnp.where(lane_lt_half, yd[0], yd[1])
                               + y_off[:, jp * LANES:(jp + 1) * LANES])
            y = jnp.concatenate(y_pairs, axis=-1) + xs * dskip_ref[:, gcol]
            y = y * _silu_of_half(z_ref[rows, gcol].astype(F32))
            ms = jnp.mean(y * y, axis=-1, keepdims=True)
            y_ref[rows, gcol] = (y * lax.rsqrt(ms + EPS) * nw_ref[:, gcol]).astype(BF16)

        return carry

    lax.fori_loop(0, z_ref.shape[0] // L, chunk, 0)
    last = slice(z_ref.shape[0] - BF16_ROWS, z_ref.shape[0])
    tailx_ref[...] = xs_ref[last, :]
    tailb_ref[...] = b_ref[last, :]
    tailc_ref[...] = c_ref[last, :]


def _ssd(proj, dt, dtT, conv_w, conv_b, a_log, d_skip, norm_w, batch, seq):
    sub = min(SSD_SUB, seq // CHUNK)
    rows = sub * CHUNK
    nc = seq // rows
    T = batch * seq
    row = lambda b, n: b * nc + n
    full = lambda shape: pl.BlockSpec(shape, lambda b, n: (0,) * len(shape))
    conv_w = 0.5 * conv_w
    conv_b = 0.5 * conv_b
    cwx, cwb, cwc = conv_w[:, :D_INNER], conv_w[:, D_INNER:D_INNER + BC_DIM], conv_w[:, D_INNER + BC_DIM:]
    cbx, cbb, cbc = conv_b[:, :D_INNER], conv_b[:, D_INNER:D_INNER + BC_DIM], conv_b[:, D_INNER + BC_DIM:]
    alog_row = _tile3_heads(a_log.reshape(1, N_SSD_HEADS))
    alog_col = a_log.reshape(N_SSD_HEADS, 1)
    dskip_row = jnp.repeat(d_skip, SSD_HEAD_DIM).reshape(1, D_INNER)
    smat = _shift_matrix(SSD_CONV, CHUNK, BF16_ROWS)
    emat = _head_expand_matrix()
    return pl.pallas_call(
        _ssd_kernel,
        grid=(batch, nc),
        in_specs=[
            pl.BlockSpec((rows, D_INNER), lambda b, n: (row(b, n), OFF_Z // D_INNER)),
            pl.BlockSpec((rows, D_INNER), lambda b, n: (row(b, n), OFF_XS // D_INNER)),
            pl.BlockSpec((rows, BC_DIM), lambda b, n: (row(b, n), OFF_B // BC_DIM)),
            pl.BlockSpec((rows, BC_DIM), lambda b, n: (row(b, n), OFF_C // BC_DIM)),
            pl.BlockSpec((rows, LANES), lambda b, n: (row(b, n), 0)),
            pl.BlockSpec((sub, N_SSD_HEADS, CHUNK), lambda b, n: (row(b, n), 0, 0)),
            full((SSD_CONV, D_INNER)), full((1, D_INNER)),
            full((SSD_CONV, BC_DIM)), full((1, BC_DIM)),
            full((SSD_CONV, BC_DIM)), full((1, BC_DIM)),
            full((1, LANES)), full((N_SSD_HEADS, 1)),
            full((1, D_INNER)), full((1, D_INNER)),
            full(smat.shape), full(emat.shape),
        ],
        out_specs=pl.BlockSpec((rows, D_INNER), lambda b, n: (row(b, n), 0)),
        out_shape=jax.ShapeDtypeStruct((T, D_INNER), BF16),
        scratch_shapes=[
            pltpu.VMEM((N_SSD_GROUPS, D_STATE, GROUP_W), F32),
            pltpu.VMEM((BF16_ROWS, D_INNER), BF16),
            pltpu.VMEM((BF16_ROWS, BC_DIM), BF16),
            pltpu.VMEM((BF16_ROWS, BC_DIM), BF16),
        ],
        compiler_params=pltpu.CompilerParams(
            dimension_semantics=("arbitrary", "arbitrary"), vmem_limit_bytes=VMEM_LIMIT_BYTES),
        name="ssd_mixer",
    )(proj, proj, proj, proj, dt, dtT, cwx, cbx, cwb, cbb, cwc, cbc,
      alog_row, alog_col, dskip_row, norm_w, smat, emat)


def _merge_kernel(x_ref, attn_ref, y_ref, ga_ref, gs_ref, bg_ref, wa_ref, ws_ref, wo_ref, h_ref):
    attn = jnp.dot(attn_ref[...], wa_ref[...], preferred_element_type=F32)
    ssd = jnp.dot(y_ref[...], ws_ref[...], preferred_element_type=F32)
    gate_a = _sigmoid(ga_ref[...].astype(F32) + bg_ref[:, :D_MODEL])
    gate_s = _sigmoid(gs_ref[...].astype(F32) + bg_ref[:, D_MODEL:])
    mixed = (gate_a * attn + gate_s * ssd).astype(BF16)
    h_ref[...] = x_ref[...] + jnp.dot(mixed, wo_ref[...], preferred_element_type=F32)


def _merge(x2d, attn, y, proj, b_gate, w_attn_o, w_ssd_o, w_out):
    T = x2d.shape[0]
    tm = min(MERGE_TM, T)
    full = lambda shape: pl.BlockSpec(shape, lambda i: (0,) * len(shape))
    return pl.pallas_call(
        _merge_kernel,
        grid=(T // tm,),
        in_specs=[
            pl.BlockSpec((tm, D_MODEL), lambda i: (i, 0)),
            pl.BlockSpec((tm, Q_DIM), lambda i: (i, 0)),
            pl.BlockSpec((tm, D_INNER), lambda i: (i, 0)),
            pl.BlockSpec((tm, D_MODEL), lambda i: (i, OFF_GA // D_MODEL)),
            pl.BlockSpec((tm, D_MODEL), lambda i: (i, OFF_GS // D_MODEL)),
            full((1, 2 * D_MODEL)),
            full((Q_DIM, D_MODEL)), full((D_INNER, D_MODEL)), full((D_MODEL, D_MODEL)),
        ],
        out_specs=pl.BlockSpec((tm, D_MODEL), lambda i: (i, 0)),
        out_shape=jax.ShapeDtypeStruct((T, D_MODEL), F32),
        compiler_params=pltpu.CompilerParams(
            dimension_semantics=("arbitrary",), vmem_limit_bytes=VMEM_LIMIT_BYTES),
        name="gated_merge",
    )(x2d, attn, y, proj, proj, b_gate, w_attn_o, w_ssd_o, w_out)


def _ffn_kernel(h_ref, n2_ref, wup_ref, cw_ref, cb_ref, wdn_ref, fn_ref, o_ref,
                buf_ref, tail_ref, act_ref):
    first = pl.program_id(1) == 0
    tm = h_ref.shape[0]
    h = h_ref[...]
    ms = jnp.mean(h * h, axis=-1, keepdims=True)
    hn = (h * lax.rsqrt(ms + EPS) * n2_ref[...]).astype(BF16)

    @pl.when(first)
    def _():
        tail_ref[...] = jnp.zeros(tail_ref.shape, F32)

    for c in range(FFN_NCHUNK):
        u = jnp.dot(hn, wup_ref[c], preferred_element_type=F32)
        buf_ref[0:SUBLANES, :] = tail_ref[c]
        buf_ref[SUBLANES:SUBLANES + tm, :] = u
        tail_ref[c] = u[tm - SUBLANES:tm, :]
        acc = u * cw_ref[c, FFN_CONV - 1:FFN_CONV, :] + cb_ref[c]
        for k in range(FFN_CONV - 1):
            off = SUBLANES - (FFN_CONV - 1) + k
            acc = acc + buf_ref[off:off + tm, :] * cw_ref[c, k:k + 1, :]
        val = acc[:, :FFN_CW]
        half_gate = acc[:, FFN_CW:]
        act_ref[:, c * FFN_CW:(c + 1) * FFN_CW] = (_silu_of_half(half_gate) * val).astype(BF16)

    h2 = h + jnp.dot(act_ref[...], wdn_ref[...], preferred_element_type=F32)
    ms2 = jnp.mean(h2 * h2, axis=-1, keepdims=True)
    o_ref[...] = h2 * lax.rsqrt(ms2 + EPS) * fn_ref[...]


def _ffn(h2d, norm2_w, w_up_r, conv_w_r, conv_b_r, w_down, final_w, batch, seq):
    tm = min(FFN_TM, seq)
    nt = seq // tm
    T = batch * seq
    full = lambda shape: pl.BlockSpec(shape, lambda b, n: (0,) * len(shape))
    return pl.pallas_call(
        _ffn_kernel,
        grid=(batch, nt),
        in_specs=[
            pl.BlockSpec((tm, D_MODEL), lambda b, n: (b * nt + n, 0)),
            full((1, D_MODEL)),
            full((FFN_NCHUNK, D_MODEL, 2 * FFN_CW)),
            full((FFN_NCHUNK, FFN_CONV, 2 * FFN_CW)),
            full((FFN_NCHUNK, 1, 2 * FFN_CW)),
            full((D_FF, D_MODEL)),
            full((1, D_MODEL)),
        ],
        out_specs=pl.BlockSpec((tm, D_MODEL), lambda b, n: (b * nt + n, 0)),
        out_shape=jax.ShapeDtypeStruct((T, D_MODEL), F32),
        scratch_shapes=[
            pltpu.VMEM((SUBLANES + tm, 2 * FFN_CW), F32),
            pltpu.VMEM((FFN_NCHUNK, SUBLANES, 2 * FFN_CW), F32),
            pltpu.VMEM((tm, D_FF), BF16),
        ],
        compiler_params=pltpu.CompilerParams(
            dimension_semantics=("arbitrary", "arbitrary"), vmem_limit_bytes=VMEM_LIMIT_BYTES),
        name="conv_ffn",
    )(h2d, norm2_w, w_up_r, conv_w_r, conv_b_r, w_down, final_w)


def _chunk_val_gate(t, gate_scale=1.0):
    lead = t.shape[:-1]
    v = t[..., :D_FF].reshape(lead + (FFN_NCHUNK, FFN_CW))
    g = (gate_scale * t[..., D_FF:]).reshape(lead + (FFN_NCHUNK, FFN_CW))
    vg = jnp.concatenate([v, g], axis=-1)
    return jnp.moveaxis(vg, -2, 0)


def kernel(x, norm1_w, w_in, b_gate, attn_sinks, w_attn_o, ssd_conv_w, ssd_conv_b, dt_bias, a_log,
           d_skip, ssd_norm_w, w_ssd_o, w_out, norm2_w, w_up, ffn_conv_w, ffn_conv_b, w_down,
           final_norm_w):
    batch, seq, _ = x.shape
    T = batch * seq
    assert norm1_w.shape[0] == 1, "single-layer kernel"
    assert seq % WINDOW == 0 and seq % CHUNK == 0

    w = w_in[0]
    o_q, o_k, o_v, o_z = 0, Q_DIM, Q_DIM + KV_DIM, Q_DIM + 2 * KV_DIM
    o_xs = o_z + D_INNER
    o_b = o_xs + D_INNER
    o_c = o_b + BC_DIM
    o_dt = o_c + BC_DIM
    o_ga = o_dt + N_SSD_HEADS
    o_gs = o_ga + D_MODEL
    cols = lambda a, n: w[:, a:a + n]
    w_q = jnp.concatenate([cols(o_q + hd * HEAD_DIM, HEAD_DIM) for hd in ATTN_HEAD_ORDER], axis=1)
    w_ao = w_attn_o[0].reshape(N_Q_HEADS, HEAD_DIM, D_MODEL)
    w_ao = jnp.concatenate([w_ao[hd] for hd in ATTN_HEAD_ORDER], axis=0).astype(BF16)
    sinks = jnp.stack([attn_sinks[0][hd] for hd in ATTN_HEAD_ORDER])
    w_main = jnp.concatenate([
        0.5 * cols(o_z, D_INNER), cols(o_xs, D_INNER), w_q, cols(o_ga, D_MODEL),
        cols(o_gs, D_MODEL), cols(o_b, BC_DIM), cols(o_c, BC_DIM), cols(o_k, KV_DIM),
        cols(o_v, KV_DIM)], axis=1).astype(BF16)
    w_dt = cols(o_dt, N_SSD_HEADS)
    w_dt_pad = _tile3_heads(w_dt).astype(BF16)
    w_dtT = w_dt.T.astype(BF16)
    dtb_row = _tile3_heads(dt_bias[0].reshape(1, N_SSD_HEADS))
    dtb_col = dt_bias[0].reshape(N_SSD_HEADS, 1)

    x2d = x.reshape(T, D_MODEL)
    proj, dt, dtT = _inproj(x2d, norm1_w[0].reshape(1, D_MODEL), w_main, w_dt_pad, w_dtT,
                            dtb_row, dtb_col)
    attn = _attention(proj, sinks, batch, seq)
    y = _ssd(proj, dt, dtT, ssd_conv_w[0], ssd_conv_b[0].reshape(1, -1), a_log[0], d_skip[0],
             ssd_norm_w[0].reshape(1, D_INNER), batch, seq)
    h = _merge(x2d, attn, y, proj, b_gate[0].reshape(1, 2 * D_MODEL), w_ao,
               w_ssd_o[0].astype(BF16), w_out[0].astype(BF16))
    out = _ffn(h, norm2_w[0].reshape(1, D_MODEL), _chunk_val_gate(w_up[0]).astype(BF16),
               _chunk_val_gate(ffn_conv_w[0], 0.5), _chunk_val_gate(ffn_conv_b[0].reshape(1, -1), 0.5),
               w_down[0].astype(BF16), final_norm_w.reshape(1, D_MODEL), batch, seq)
    return out.reshape(batch, seq, D_MODEL)
```

```python
import functools

import jax
import jax.numpy as jnp
from jax import lax
from jax.experimental import pallas as pl
from jax.experimental.pallas import tpu as pltpu

F32 = jnp.float32
BF16 = jnp.bfloat16

D_MODEL = 1024
N_Q_HEADS = 16
N_KV_HEADS = 4
Q_PER_KV = N_Q_HEADS // N_KV_HEADS
HEAD_DIM = 64
WINDOW = 128
D_INNER = 2048
SSD_HEAD_DIM = 64
N_SSD_HEADS = 32
N_SSD_GROUPS = 4
HEADS_PER_GROUP = N_SSD_HEADS // N_SSD_GROUPS
D_STATE = 128
SSD_CONV = 4
CHUNK = 128
D_FF = 2816
FFN_CONV = 3
EPS = 1e-5
NEG = -1e30
LOG2E = 1.4426950408889634
Q_DIM = N_Q_HEADS * HEAD_DIM
KV_DIM = N_KV_HEADS * HEAD_DIM
BC_DIM = N_SSD_GROUPS * D_STATE
GROUP_W = D_INNER // N_SSD_GROUPS
PROJ_W = 2 * D_INNER + Q_DIM + 2 * D_MODEL + 2 * BC_DIM + 2 * KV_DIM

LANES = 128
SUBLANES = 8
BF16_ROWS = 16
VMEM_LIMIT_BYTES = 62 * 1024 * 1024

TILE = 512
SUB = TILE // CHUNK
assert SUB == N_SSD_GROUPS and WINDOW == CHUNK
PCW = PROJ_W // SUB
PC_Z = 0
PC_XS = PC_Z + GROUP_W
PC_Q = PC_XS + GROUP_W
PC_GA = PC_Q + 2 * LANES
PC_GS = PC_GA + D_MODEL // SUB
PC_B = PC_GS + D_MODEL // SUB
PC_C = PC_B + D_STATE
PC_KV = PC_C + D_STATE
assert PC_KV + LANES == PCW
PROD_PIECE = 256
PROD_SLICES = tuple((a, min(a + PROD_PIECE, PCW)) for a in range(0, PCW, PROD_PIECE))
ATTN_PIECES_AFTER = (1, 4, 7)

MERGE_TM = 512
FFN_TM = 512
FFN_CW = 256
FFN_NCHUNK = D_FF // FFN_CW

ATTN_HEAD_ORDER = tuple(
    (2 * (j // Q_PER_KV) + half) * Q_PER_KV + j % Q_PER_KV
    for j in range(N_Q_HEADS // 2) for half in range(2))


def _silu_of_half(h):
    return h + h * jnp.tanh(h)


def _sigmoid(x):
    return 1.0 / (1.0 + jnp.exp(-x))


def _softplus(x):
    return jnp.maximum(x, 0.0) + jnp.log(1.0 + jnp.exp(-jnp.abs(x)))


def _tile3_heads(t):
    pad = jnp.zeros(t.shape[:-1] + (LANES - 3 * N_SSD_HEADS,), t.dtype)
    return jnp.concatenate([t, t, t, pad], axis=-1)


def _split3(x):
    hi = x.astype(BF16)
    r1 = x - hi.astype(F32)
    mid = r1.astype(BF16)
    lo = (r1 - mid.astype(F32)).astype(BF16)
    return hi, mid, lo


def _resident(shape):
    return pl.BlockSpec(shape, lambda *_: (0,) * len(shape), pipeline_mode=pl.Buffered(1))


def _shift_matrix(taps, L, tail):
    r = jnp.arange((taps - 1) * L)[:, None]
    c = jnp.arange(tail + L)[None, :]
    return (c == (r % L) + tail - (taps - 1 - r // L)).astype(BF16)


def _head_expand_matrix():
    k = jnp.arange(LANES)[:, None]
    c = jnp.arange(D_INNER)[None, :]
    return jnp.logical_and(k < 3 * N_SSD_HEADS, k % N_SSD_HEADS == c // SSD_HEAD_DIM).astype(BF16)


def _split3_lanes(v, lane):
    hi = v.astype(BF16).astype(F32)
    r1 = v - hi
    mid = r1.astype(BF16).astype(F32)
    parts = jnp.where(lane < N_SSD_HEADS, hi, jnp.where(lane < 2 * N_SSD_HEADS, mid, r1 - mid))
    return parts.astype(BF16)


def _causal_conv(raw, tail, smat, w, b, taps):
    L = raw.shape[0]
    ext = jnp.concatenate([tail, raw], axis=0)
    sh = jnp.dot(smat, ext, preferred_element_type=F32)
    acc = raw.astype(F32) * w[taps - 1:taps, :] + b
    for k in range(taps - 1):
        acc = acc + sh[k * L:(k + 1) * L, :] * w[k:k + 1, :]
    return acc


def _mixer_kernel(sinks_ref, x_ref, nw_ref, w_ref, wdt_ref, wdtT_ref, dtb_ref, dtbT_ref,
                  cwx_ref, cbx_ref, cwb_ref, cbb_ref, cwc_ref, cbc_ref,
                  alog_ref, alogT_ref, dskip_ref, ynw_ref, smat_ref, emat_ref,
                  attn_ref, y_ref, gates_ref,
                  proj_a, proj_b, dt_a, dt_b, dtT_a, dtT_b, xn_ref,
                  state_ref, tailx_ref, tailb_ref, tailc_ref, kvprev_ref, *, tiles_per_seq):
    s = pl.program_id(0)
    L = CHUNK
    seq_start = (s + tiles_per_seq - 1) % tiles_per_seq == 0

    @pl.when(s == 0)
    def _():
        proj_b[...] = jnp.zeros(proj_b.shape, BF16)
        dt_b[...] = jnp.ones(dt_b.shape, F32)
        dtT_b[...] = jnp.ones(dtT_b.shape, F32)

    @pl.when(jnp.logical_or(seq_start, s == 0))
    def _():
        state_ref[...] = jnp.zeros(state_ref.shape, F32)
        tailx_ref[...] = jnp.zeros(tailx_ref.shape, BF16)
        tailb_ref[...] = jnp.zeros(tailb_ref.shape, BF16)
        tailc_ref[...] = jnp.zeros(tailc_ref.shape, BF16)
        kvprev_ref[...] = jnp.zeros(kvprev_ref.shape, BF16)

    x = x_ref[...]
    ms = jnp.mean(x * x, axis=-1, keepdims=True)
    xn_ref[...] = (x * lax.rsqrt(ms + EPS) * nw_ref[...]).astype(BF16)

    qi = lax.broadcasted_iota(jnp.int32, (L, 2 * L), 0)
    si = lax.broadcasted_iota(jnp.int32, (L, 2 * L), 1)
    in_prev = jnp.logical_and(si < L, si > qi)
    in_cur = jnp.logical_and(si >= L, si - L <= qi)
    lane = lax.broadcasted_iota(jnp.int32, (L, LANES), 1)
    left = lane < HEAD_DIM
    ri = lax.broadcasted_iota(jnp.int32, (L, L), 0)
    ci = lax.broadcasted_iota(jnp.int32, (L, L), 1)
    causal = ci <= ri

    def step(pr, pw, dt_r, dt_w, dtT_r, dtT_w):
        xn_all = xn_ref[...]
        dt_w[...] = _softplus(jnp.dot(xn_all, wdt_ref[...], preferred_element_type=F32) + dtb_ref[...])
        dtT_new = _softplus(lax.dot_general(wdtT_ref[...], xn_all, (((1,), (1,)), ((), ())),
                                            preferred_element_type=F32) + dtbT_ref[...])
        for c in range(SUB):
            dtT_w[c] = dtT_new[:, c * L:(c + 1) * L]

        def chunk(i, carry):
            r0 = pl.multiple_of(i * L, L)
            rows = pl.ds(r0, L)
            before_tail = pl.ds(pl.multiple_of(jnp.maximum(r0 - BF16_ROWS, 0), BF16_ROWS), BF16_ROWS)
            before_blk = pl.ds(pl.multiple_of(jnp.maximum(r0 - L, 0), L), L)
            inner = i > 0

            pieces = list(PROD_SLICES)

            def produce():
                a, b = pieces.pop(0)
                pw[i, :, a:b] = jnp.dot(xn_ref[...], w_ref[i, :, a:b],
                                        preferred_element_type=F32).astype(BF16)

            valid =jnp.logical_or(jnp.logical_and(in_prev, jnp.logical_or(inner, jnp.logical_not(seq_start))),
                                   in_cur)
            kvc = slice(PC_KV, PC_KV + LANES)
            for kvp in range(N_KV_HEADS // 2):
                k_prev = jnp.where(inner, pr[kvp, before_blk, kvc], kvprev_ref[:, kvp * LANES:(kvp + 1) * LANES])
                v_prev = jnp.where(inner, pr[2 + kvp, before_blk, kvc],
                                   kvprev_ref[:, (2 + kvp) * LANES:(3 + kvp) * LANES])
                kpair = jnp.concatenate([k_prev, pr[kvp, rows, kvc]], axis=0)
                vpair = jnp.concatenate([v_prev, pr[2 + kvp, rows, kvc]], axis=0)
                for g in range(Q_PER_KV):
                    j = kvp * Q_PER_KV + g
                    qcol = slice(PC_Q + (j % 2) * LANES, PC_Q + (j % 2 + 1) * LANES)
                    qc = pr[j // 2, rows, qcol].astype(F32) * (HEAD_DIM ** -0.5)
                    q2 = jnp.concatenate([jnp.where(left, qc, 0.0), jnp.where(left, 0.0, qc)],
                                         axis=0).astype(BF16)
                    s2 = lax.dot_general(q2, kpair, (((1,), (1,)), ((), ())),
                                         preferred_element_type=F32)
                    ps, rs = [], []
                    for half in range(2):
                        sc = jnp.where(valid, s2[half * L:(half + 1) * L, :], NEG)
                        sink = sinks_ref[2 * j + half]
                        m = jnp.maximum(jnp.max(sc, axis=-1, keepdims=True), sink)
                        p = jnp.exp(sc - m)
                        denom = jnp.sum(p, axis=-1, keepdims=True) + jnp.exp(sink - m)
                        ps.append(p.astype(BF16))
                        rs.append(1.0 / denom)
                    o2 = jnp.dot(jnp.concatenate(ps, axis=0), vpair, preferred_element_type=F32)
                    o = jnp.where(left, o2[:L, :] * rs[0], o2[L:, :] * rs[1])
                    attn_ref[rows, j * LANES:(j + 1) * LANES] = o.astype(BF16)
                    if j in ATTN_PIECES_AFTER:
                        produce()

            smat = smat_ref[...]
            dt = dt_r[rows, :]
            dtT = dtT_r[i]
            dA = dt * (-jnp.exp(alog_ref[...]))
            dAT = dtT * (-jnp.exp(alogT_ref[...]))
            tri = jnp.where(causal, 1.0, 0.0).astype(BF16)
            triT = jnp.where(ri <= ci, 1.0, 0.0).astype(BF16)
            a_cs = sum(jnp.dot(tri, p, preferred_element_type=F32) for p in _split3(dA))
            a_csT = sum(jnp.dot(p, triT, preferred_element_type=F32) for p in _split3(dAT))
            a_last = a_cs[L - 1:L, :]
            ea = jnp.exp(a_cs)
            w_state = dt * jnp.exp(a_last - a_cs)
            a2 = a_cs * LOG2E
            a2T = (a_csT - jnp.log(dtT)) * LOG2E
            ea_parts = _split3_lanes(ea, lane)
            ws_parts = _split3_lanes(w_state, lane)

            for g in range(N_SSD_GROUPS):
                gcol = slice(g * GROUP_W, (g + 1) * GROUP_W)
                ncol = slice(g * D_STATE, (g + 1) * D_STATE)
                xcol = slice(PC_XS, PC_XS + GROUP_W)
                bcol = slice(PC_B, PC_B + D_STATE)
                ccol = slice(PC_C, PC_C + D_STATE)
                xs = _silu_of_half(_causal_conv(
                    pr[g, rows, xcol], jnp.where(inner, pr[g, before_tail, xcol], tailx_ref[:, gcol]),
                    smat, cwx_ref[:, gcol], cbx_ref[:, gcol], SSD_CONV))
                bg = _silu_of_half(_causal_conv(
                    pr[g, rows, bcol], jnp.where(inner, pr[g, before_tail, bcol], tailb_ref[:, ncol]),
                    smat, cwb_ref[:, ncol], cbb_ref[:, ncol], SSD_CONV))
                cg = _silu_of_half(_causal_conv(
                    pr[g, rows, ccol], jnp.where(inner, pr[g, before_tail, ccol], tailc_ref[:, ncol]),
                    smat, cwc_ref[:, ncol], cbc_ref[:, ncol], SSD_CONV))
                xs_bf = xs.astype(BF16)
                bg_bf = bg.astype(BF16)
                cg_bf = cg.astype(BF16)
                cb = lax.dot_general(cg_bf, bg_bf, (((1,), (1,)), ((), ())), preferred_element_type=F32)
                ea_g = jnp.dot(ea_parts, emat_ref[:, gcol], preferred_element_type=F32)
                ws_g = jnp.dot(ws_parts, emat_ref[:, gcol], preferred_element_type=F32)
                prev = state_ref[g]
                y_off = jnp.dot(cg_bf, prev.astype(BF16), preferred_element_type=F32) * ea_g
                xw = (xs * ws_g).astype(BF16)
                new_states = jnp.dot(bg.T.astype(BF16), xw, preferred_element_type=F32)
                state_ref[g] = prev * ea_g[L - 1:L, :] + new_states
                produce()
                y_pairs = []
                for jp in range(HEADS_PER_GROUP // 2):
                    h0 = g * HEADS_PER_GROUP + 2 * jp
                    xs_pair_bf = xs_bf[:, jp * LANES:(jp + 1) * LANES]
                    yd = []
                    for h in (h0, h0 + 1):
                        seg2 = a2[:, h:h + 1] - a2T[h:h + 1, :]
                        m = cb * jnp.exp2(jnp.where(causal, seg2, NEG))
                        yd.append(jnp.dot(m.astype(BF16), xs_pair_bf, preferred_element_type=F32))
                    y_pairs.append(jnp.where(left, yd[0], yd[1]) + y_off[:, jp * LANES:(jp + 1) * LANES])
                y = jnp.concatenate(y_pairs, axis=-1) + xs * dskip_ref[:, gcol]
                y = y * _silu_of_half(pr[g, rows, PC_Z:PC_Z + GROUP_W].astype(F32))
                ms_y = jnp.mean(y * y, axis=-1, keepdims=True)
                y_ref[rows, gcol] = (y * lax.rsqrt(ms_y + EPS) * ynw_ref[:, gcol]).astype(BF16)
                if g < 2:
                    produce()
            assert not pieces
            return carry

        lax.fori_loop(0, SUB, chunk, 0)

        last_tail = slice(TILE - BF16_ROWS, TILE)
        last_blk = slice(TILE - L, TILE)
        gw = D_MODEL // SUB
        for g in range(SUB):
            tailx_ref[:, g * GROUP_W:(g + 1) * GROUP_W] = pr[g, last_tail, PC_XS:PC_XS + GROUP_W]
            tailb_ref[:, g * D_STATE:(g + 1) * D_STATE] = pr[g, last_tail, PC_B:PC_B + D_STATE]
            tailc_ref[:, g * D_STATE:(g + 1) * D_STATE] = pr[g, last_tail, PC_C:PC_C + D_STATE]
            kvprev_ref[:, g * LANES:(g + 1) * LANES] = pr[g, last_blk, PC_KV:PC_KV + LANES]
            gates_ref[:, g * gw:(g + 1) * gw] = pr[g, :, PC_GA:PC_GA + gw]
            gates_ref[:, D_MODEL + g * gw:D_MODEL + (g + 1) * gw] = pr[g, :, PC_GS:PC_GS + gw]

    @pl.when(s % 2 == 0)
    def _():
        step(proj_b, proj_a, dt_b, dt_a, dtT_b, dtT_a)

    @pl.when(s % 2 == 1)
    def _():
        step(proj_a, proj_b, dt_a, dt_b, dtT_a, dtT_b)


def _mixer(x2d, norm_w, w4, w_dt, w_dtT, dt_bias_row, dt_bias_col, sinks_ordered,
           conv_w, conv_b, a_log, d_skip, ssd_norm_w, seq):
    T = x2d.shape[0]
    n_tiles = T // TILE
    conv_w = 0.5 * conv_w
    conv_b = 0.5 * conv_b
    cwx, cwb, cwc = conv_w[:, :D_INNER], conv_w[:, D_INNER:D_INNER + BC_DIM], conv_w[:, D_INNER + BC_DIM:]
    cbx, cbb, cbc = conv_b[:, :D_INNER], conv_b[:, D_INNER:D_INNER + BC_DIM], conv_b[:, D_INNER + BC_DIM:]
    alog_row = _tile3_heads(a_log.reshape(1, N_SSD_HEADS))
    alog_col = a_log.reshape(N_SSD_HEADS, 1)
    dskip_row = jnp.repeat(d_skip, SSD_HEAD_DIM).reshape(1, D_INNER)
    smat = _shift_matrix(SSD_CONV, CHUNK, BF16_ROWS)
    emat = _head_expand_matrix()
    consumed = lambda s: (jnp.maximum(s - 1, 0), 0)
    return pl.pallas_call(
        functools.partial(_mixer_kernel, tiles_per_seq=seq // TILE),
        grid=(n_tiles + 1,),
        in_specs=[
            pl.BlockSpec(memory_space=pltpu.SMEM),
            pl.BlockSpec((TILE, D_MODEL), lambda s: (jnp.minimum(s, n_tiles - 1), 0)),
            _resident((1, D_MODEL)),
            _resident((SUB, D_MODEL, PCW)),
            _resident((D_MODEL, LANES)),
            _resident((N_SSD_HEADS, D_MODEL)),
            _resident((1, LANES)),
            _resident((N_SSD_HEADS, 1)),
            _resident((SSD_CONV, D_INNER)), _resident((1, D_INNER)),
            _resident((SSD_CONV, BC_DIM)), _resident((1, BC_DIM)),
            _resident((SSD_CONV, BC_DIM)), _resident((1, BC_DIM)),
            _resident((1, LANES)), _resident((N_SSD_HEADS, 1)),
            _resident((1, D_INNER)), _resident((1, D_INNER)),
            _resident(smat.shape), _resident(emat.shape),
        ],
        out_specs=[
            pl.BlockSpec((TILE, Q_DIM), consumed),
            pl.BlockSpec((TILE, D_INNER), consumed),
            pl.BlockSpec((TILE, 2 * D_MODEL), consumed),
        ],
        out_shape=[
            jax.ShapeDtypeStruct((T, Q_DIM), BF16),
            jax.ShapeDtypeStruct((T, D_INNER), BF16),
            jax.ShapeDtypeStruct((T, 2 * D_MODEL), BF16),
        ],
        scratch_shapes=[
            pltpu.VMEM((SUB, TILE, PCW), BF16), pltpu.VMEM((SUB, TILE, PCW), BF16),
            pltpu.VMEM((TILE, LANES), F32), pltpu.VMEM((TILE, LANES), F32),
            pltpu.VMEM((SUB, N_SSD_HEADS, CHUNK), F32), pltpu.VMEM((SUB, N_SSD_HEADS, CHUNK), F32),
            pltpu.VMEM((TILE, D_MODEL), BF16),
            pltpu.VMEM((N_SSD_GROUPS, D_STATE, GROUP_W), F32),
            pltpu.VMEM((BF16_ROWS, D_INNER), BF16),
            pltpu.VMEM((BF16_ROWS, BC_DIM), BF16),
            pltpu.VMEM((BF16_ROWS, BC_DIM), BF16),
            pltpu.VMEM((WINDOW, 2 * KV_DIM), BF16),
        ],
        compiler_params=pltpu.CompilerParams(
            dimension_semantics=("arbitrary",), vmem_limit_bytes=VMEM_LIMIT_BYTES),
        name="token_mixers",
    )(sinks_ordered, x2d, norm_w, w4, w_dt, w_dtT, dt_bias_row, dt_bias_col,
      cwx, cbx, cwb, cbb, cwc, cbc, alog_row, alog_col, dskip_row, ssd_norm_w, smat, emat)


def _merge_kernel(x_ref, attn_ref, y_ref, ga_ref, gs_ref, bg_ref, wa_ref, ws_ref, wo_ref, h_ref):
    attn = jnp.dot(attn_ref[...], wa_ref[...], preferred_element_type=F32)
    ssd = jnp.dot(y_ref[...], ws_ref[...], preferred_element_type=F32)
    gate_a = _sigmoid(ga_ref[...].astype(F32) + bg_ref[:, :D_MODEL])
    gate_s = _sigmoid(gs_ref[...].astype(F32) + bg_ref[:, D_MODEL:])
    mixed = (gate_a * attn + gate_s * ssd).astype(BF16)
    h_ref[...] = x_ref[...] + jnp.dot(mixed, wo_ref[...], preferred_element_type=F32)


def _merge(x2d, attn, y, gates, b_gate, w_attn_o, w_ssd_o, w_out):
    T = x2d.shape[0]
    tm = min(MERGE_TM, T)
    return pl.pallas_call(
        _merge_kernel,
        grid=(T // tm,),
        in_specs=[
            pl.BlockSpec((tm, D_MODEL), lambda i: (i, 0)),
            pl.BlockSpec((tm, Q_DIM), lambda i: (i, 0)),
            pl.BlockSpec((tm, D_INNER), lambda i: (i, 0)),
            pl.BlockSpec((tm, D_MODEL), lambda i: (i, 0)),
            pl.BlockSpec((tm, D_MODEL), lambda i: (i, 1)),
            _resident((1, 2 * D_MODEL)),
            _resident((Q_DIM, D_MODEL)), _resident((D_INNER, D_MODEL)), _resident((D_MODEL, D_MODEL)),
        ],
        out_specs=pl.BlockSpec((tm, D_MODEL), lambda i: (i, 0)),
        out_shape=jax.ShapeDtypeStruct((T, D_MODEL), F32),
        compiler_params=pltpu.CompilerParams(
            dimension_semantics=("arbitrary",), vmem_limit_bytes=VMEM_LIMIT_BYTES),
        name="gated_merge",
    )(x2d, attn, y, gates, gates, b_gate, w_attn_o, w_ssd_o, w_out)


def _ffn_kernel(h_ref, n2_ref, wup_ref, cw_ref, cb_ref, wdn_ref, fn_ref, o_ref,
                buf_ref, tail_ref, act_ref):
    first = pl.program_id(1) == 0
    tm = h_ref.shape[0]
    h = h_ref[...]
    ms = jnp.mean(h * h, axis=-1, keepdims=True)
    hn = (h * lax.rsqrt(ms + EPS) * n2_ref[...]).astype(BF16)

    @pl.when(first)
    def _():
        tail_ref[...] = jnp.zeros(tail_ref.shape, F32)

    for c in range(FFN_NCHUNK):
        u = jnp.dot(hn, wup_ref[c], preferred_element_type=F32)
        buf_ref[0:SUBLANES, :] = tail_ref[c]
        buf_ref[SUBLANES:SUBLANES + tm, :] = u
        tail_ref[c] = u[tm - SUBLANES:tm, :]
        acc = u * cw_ref[c, FFN_CONV - 1:FFN_CONV, :] + cb_ref[c]
        for k in range(FFN_CONV - 1):
            off = SUBLANES - (FFN_CONV - 1) + k
            acc = acc + buf_ref[off:off + tm, :] * cw_ref[c, k:k + 1, :]
        val = acc[:, :FFN_CW]
        half_gate = acc[:, FFN_CW:]
        act_ref[:, c * FFN_CW:(c + 1) * FFN_CW] = (_silu_of_half(half_gate) * val).astype(BF16)

    h2 = h + jnp.dot(act_ref[...], wdn_ref[...], preferred_element_type=F32)
    ms2 = jnp.mean(h2 * h2, axis=-1, keepdims=True)
    o_ref[...] = h2 * lax.rsqrt(ms2 + EPS) * fn_ref[...]


def _ffn(h2d, norm2_w, w_up_r, conv_w_r, conv_b_r, w_down, final_w, batch, seq):
    tm = min(FFN_TM, seq)
    nt = seq // tm
    T = batch * seq
    full = lambda shape: pl.BlockSpec(shape, lambda b, n: (0,) * len(shape))
    return pl.pallas_call(
        _ffn_kernel,
        grid=(batch, nt),
        in_specs=[
            pl.BlockSpec((tm, D_MODEL), lambda b, n: (b * nt + n, 0)),
            full((1, D_MODEL)),
            full((FFN_NCHUNK, D_MODEL, 2 * FFN_CW)),
            full((FFN_NCHUNK, FFN_CONV, 2 * FFN_CW)),
            full((FFN_NCHUNK, 1, 2 * FFN_CW)),
            full((D_FF, D_MODEL)),
            full((1, D_MODEL)),
        ],
        out_specs=pl.BlockSpec((tm, D_MODEL), lambda b, n: (b * nt + n, 0)),
        out_shape=jax.ShapeDtypeStruct((T, D_MODEL), F32),
        scratch_shapes=[
            pltpu.VMEM((SUBLANES + tm, 2 * FFN_CW), F32),
            pltpu.VMEM((FFN_NCHUNK, SUBLANES, 2 * FFN_CW), F32),
            pltpu.VMEM((tm, D_FF), BF16),
        ],
        compiler_params=pltpu.CompilerParams(
            dimension_semantics=("arbitrary", "arbitrary"), vmem_limit_bytes=VMEM_LIMIT_BYTES),
        name="conv_ffn",
    )(h2d, norm2_w, w_up_r, conv_w_r, conv_b_r, w_down, final_w)


def _chunk_val_gate(t, gate_scale=1.0):
    lead = t.shape[:-1]
    v = t[..., :D_FF].reshape(lead + (FFN_NCHUNK, FFN_CW))
    g = (gate_scale * t[..., D_FF:]).reshape(lead + (FFN_NCHUNK, FFN_CW))
    vg = jnp.concatenate([v, g], axis=-1)
    return jnp.moveaxis(vg, -2, 0)


def _projection_weight_chunks(w):
    o_q, o_k, o_v, o_z = 0, Q_DIM, Q_DIM + KV_DIM, Q_DIM + 2 * KV_DIM
    o_xs = o_z + D_INNER
    o_b = o_xs + D_INNER
    o_c = o_b + BC_DIM
    o_dt = o_c + BC_DIM
    o_ga = o_dt + N_SSD_HEADS
    o_gs = o_ga + D_MODEL
    cols = lambda a, n: w[:, a:a + n]
    w_q = jnp.concatenate([cols(o_q + hd * HEAD_DIM, HEAD_DIM) for hd in ATTN_HEAD_ORDER], axis=1)
    kv_pairs = (cols(o_k, LANES), cols(o_k + LANES, LANES), cols(o_v, LANES), cols(o_v + LANES, LANES))
    gw = D_MODEL // SUB
    chunks = []
    for g in range(SUB):
        chunks.append(jnp.concatenate([
            0.5 * cols(o_z + g * GROUP_W, GROUP_W),
            cols(o_xs + g * GROUP_W, GROUP_W),
            w_q[:, g * 2 * LANES:(g + 1) * 2 * LANES],
            cols(o_ga + g * gw, gw),
            cols(o_gs + g * gw, gw),
            cols(o_b + g * D_STATE, D_STATE),
            cols(o_c + g * D_STATE, D_STATE),
            kv_pairs[g]], axis=1))
    return jnp.stack(chunks).astype(BF16), cols(o_dt, N_SSD_HEADS)


def kernel(x, norm1_w, w_in, b_gate, attn_sinks, w_attn_o, ssd_conv_w, ssd_conv_b, dt_bias, a_log,
           d_skip, ssd_norm_w, w_ssd_o, w_out, norm2_w, w_up, ffn_conv_w, ffn_conv_b, w_down,
           final_norm_w):
    batch, seq, _ = x.shape
    T = batch * seq
    assert norm1_w.shape[0] == 1, "single-layer kernel"
    assert seq % TILE == 0

    w4, w_dt = _projection_weight_chunks(w_in[0])
    w_dt_pad = _tile3_heads(w_dt).astype(BF16)
    w_dtT = w_dt.T.astype(BF16)
    dtb_row = _tile3_heads(dt_bias[0].reshape(1, N_SSD_HEADS))
    dtb_col = dt_bias[0].reshape(N_SSD_HEADS, 1)
    w_ao = w_attn_o[0].reshape(N_Q_HEADS, HEAD_DIM, D_MODEL)
    w_ao = jnp.concatenate([w_ao[hd] for hd in ATTN_HEAD_ORDER], axis=0).astype(BF16)
    sinks = jnp.stack([attn_sinks[0][hd] for hd in ATTN_HEAD_ORDER])

    x2d = x.reshape(T, D_MODEL)
    attn, y, gates = _mixer(x2d, norm1_w[0].reshape(1, D_MODEL), w4, w_dt_pad, w_dtT, dtb_row, dtb_col,
                            sinks, ssd_conv_w[0], ssd_conv_b[0].reshape(1, -1), a_log[0], d_skip[0],
                            ssd_norm_w[0].reshape(1, D_INNER), seq)
    h = _merge(x2d, attn, y, gates, b_gate[0].reshape(1, 2 * D_MODEL), w_ao,
               w_ssd_o[0].astype(BF16), w_out[0].astype(BF16))
    out = _ffn(h, norm2_w[0].reshape(1, D_MODEL), _chunk_val_gate(w_up[0]).astype(BF16),
               _chunk_val_gate(ffn_conv_w[0], 0.5), _chunk_val_gate(ffn_conv_b[0].reshape(1, -1), 0.5),
               w_down[0].astype(BF16), final_norm_w.reshape(1, D_MODEL), batch, seq)
    return out.reshape(batch, seq, D_MODEL)
```

```python
import functools

import jax
import jax.numpy as jnp
from jax import lax
from jax.experimental import pallas as pl
from jax.experimental.pallas import tpu as pltpu

F32 = jnp.float32
BF16 = jnp.bfloat16

D_MODEL = 1024
N_Q_HEADS = 16
N_KV_HEADS = 4
Q_PER_KV = N_Q_HEADS // N_KV_HEADS
HEAD_DIM = 64
WINDOW = 128
D_INNER = 2048
SSD_HEAD_DIM = 64
N_SSD_HEADS = 32
N_SSD_GROUPS = 4
HEADS_PER_GROUP = N_SSD_HEADS // N_SSD_GROUPS
D_STATE = 128
SSD_CONV = 4
CHUNK = 128
D_FF = 2816
FFN_CONV = 3
EPS = 1e-5
NEG = -1e30
LOG2E = 1.4426950408889634
Q_DIM = N_Q_HEADS * HEAD_DIM
KV_DIM = N_KV_HEADS * HEAD_DIM
BC_DIM = N_SSD_GROUPS * D_STATE
XBC_DIM = D_INNER + 2 * BC_DIM
GROUP_W = D_INNER // N_SSD_GROUPS

LANES = 128
SUBLANES = 8
VMEM_LIMIT_BYTES = 56 * 1024 * 1024

OFF_Z = 0
OFF_XS = OFF_Z + D_INNER
OFF_Q = OFF_XS + D_INNER
OFF_GA = OFF_Q + Q_DIM
OFF_GS = OFF_GA + D_MODEL
OFF_B = OFF_GS + D_MODEL
OFF_C = OFF_B + BC_DIM
OFF_K = OFF_C + BC_DIM
OFF_V = OFF_K + KV_DIM
PROJ_W = OFF_V + KV_DIM

IN_TM = 512
IN_TN = 512
CONV_CHUNKS = {OFF_XS // IN_TN + k: k for k in range(D_INNER // IN_TN)}
CONV_CHUNKS[OFF_B // IN_TN] = D_INNER // IN_TN
CONV_CHUNKS[OFF_C // IN_TN] = D_INNER // IN_TN + 1
ATTN_ROWS = 512
SSD_SUB = 4
MERGE_TM = 512
FFN_TM = 512
FFN_CW = 256
FFN_NCHUNK = D_FF // FFN_CW


def _silu_of_half(h):
    return h + h * jnp.tanh(h)


def _sigmoid(x):
    return 1.0 / (1.0 + jnp.exp(-x))


def _softplus(x):
    return jnp.maximum(x, 0.0) + jnp.log(1.0 + jnp.exp(-jnp.abs(x)))


def _tile3_heads(t):
    pad = jnp.zeros(t.shape[:-1] + (LANES - 3 * N_SSD_HEADS,), t.dtype)
    return jnp.concatenate([t, t, t, pad], axis=-1)


def _split3(x):
    hi = x.astype(BF16)
    r1 = x - hi.astype(F32)
    mid = r1.astype(BF16)
    lo = (r1 - mid.astype(F32)).astype(BF16)
    return hi, mid, lo


def _resident(shape):
    return pl.BlockSpec(shape, lambda *_: (0,) * len(shape), pipeline_mode=pl.Buffered(1))


def _conv_rows(u, buf_ref, tail_ref, w, b, taps):
    tm = u.shape[0]
    buf_ref[0:SUBLANES, :] = tail_ref[...]
    buf_ref[SUBLANES:SUBLANES + tm, :] = u
    tail_ref[...] = u[tm - SUBLANES:tm, :]
    acc = u * w[taps - 1:taps, :] + b
    for k in range(taps - 1):
        off = SUBLANES - (taps - 1) + k
        acc = acc + buf_ref[off:off + tm, :] * w[k:k + 1, :]
    return acc


def _inproj_kernel(x_ref, nw_ref, w_ref, cw_ref, cb_ref, wdt_ref, wdtT_ref, dtb_ref, dtbT_ref,
                   proj_ref, dt_ref, dtT_ref, buf_ref, tail_ref, *, tiles_per_seq):
    @pl.when(pl.program_id(0) % tiles_per_seq == 0)
    def _():
        tail_ref[...] = jnp.zeros(tail_ref.shape, F32)

    x = x_ref[...]
    ms = jnp.mean(x * x, axis=-1, keepdims=True)
    xn = (x * lax.rsqrt(ms + EPS) * nw_ref[...]).astype(BF16)
    for c in range(PROJ_W // IN_TN):
        cs = slice(c * IN_TN, (c + 1) * IN_TN)
        u = jnp.dot(xn, w_ref[:, cs], preferred_element_type=F32)
        if c in CONV_CHUNKS:
            k = CONV_CHUNKS[c]
            ks = slice(k * IN_TN, (k + 1) * IN_TN)
            u = _silu_of_half(_conv_rows(u, buf_ref, tail_ref.at[k], cw_ref[:, ks], cb_ref[:, ks], SSD_CONV))
        proj_ref[:, cs] = u.astype(BF16)
    dt_raw = jnp.dot(xn, wdt_ref[...], preferred_element_type=F32)
    dt_ref[...] = _softplus(dt_raw + dtb_ref[...])
    dtT_raw = lax.dot_general(wdtT_ref[...], xn, (((1,), (1,)), ((), ())),
                              preferred_element_type=F32)
    dtT = _softplus(dtT_raw + dtbT_ref[...])
    for c in range(dtT_ref.shape[0]):
        dtT_ref[c] = dtT[:, c * CHUNK:(c + 1) * CHUNK]


def _inproj(x2d, norm_w, w_main, conv_w_half, conv_b_half, w_dt, w_dtT, dt_bias_row, dt_bias_col, seq):
    T = x2d.shape[0]
    tm = min(IN_TM, seq)
    return pl.pallas_call(
        functools.partial(_inproj_kernel, tiles_per_seq=seq // tm),
        grid=(T // tm,),
        in_specs=[
            pl.BlockSpec((tm, D_MODEL), lambda i: (i, 0)),
            _resident((1, D_MODEL)),
            _resident((D_MODEL, PROJ_W)),
            _resident((SSD_CONV, XBC_DIM)),
            _resident((1, XBC_DIM)),
            _resident((D_MODEL, LANES)),
            _resident((N_SSD_HEADS, D_MODEL)),
            _resident((1, LANES)),
            _resident((N_SSD_HEADS, 1)),
        ],
        out_specs=[
            pl.BlockSpec((tm, PROJ_W), lambda i: (i, 0)),
            pl.BlockSpec((tm, LANES), lambda i: (i, 0)),
            pl.BlockSpec((tm // CHUNK, N_SSD_HEADS, CHUNK), lambda i: (i, 0, 0)),
        ],
        out_shape=[
            jax.ShapeDtypeStruct((T, PROJ_W), BF16),
            jax.ShapeDtypeStruct((T, LANES), F32),
            jax.ShapeDtypeStruct((T // CHUNK, N_SSD_HEADS, CHUNK), F32),
        ],
        scratch_shapes=[
            pltpu.VMEM((SUBLANES + tm, IN_TN), F32),
            pltpu.VMEM((XBC_DIM // IN_TN, SUBLANES, IN_TN), F32),
        ],
        compiler_params=pltpu.CompilerParams(
            dimension_semantics=("arbitrary",), vmem_limit_bytes=VMEM_LIMIT_BYTES),
        name="inproj",
    )(x2d, norm_w, w_main, conv_w_half, conv_b_half, w_dt, w_dtT, dt_bias_row, dt_bias_col)


ATTN_HEAD_ORDER = tuple(
    (2 * (j // Q_PER_KV) + half) * Q_PER_KV + j % Q_PER_KV
    for j in range(N_Q_HEADS // 2) for half in range(2))


def _attn_kernel(sinks_ref, q_ref, kp_ref, kc_ref, vp_ref, vc_ref, o_ref):
    W = WINDOW
    nsub = q_ref.shape[0] // W
    qi = lax.broadcasted_iota(jnp.int32, (W, 2 * W), 0)
    si = lax.broadcasted_iota(jnp.int32, (W, 2 * W), 1)
    in_prev = jnp.logical_and(si < W, si > qi)
    in_cur = jnp.logical_and(si >= W, si - W <= qi)
    lane = lax.broadcasted_iota(jnp.int32, (W, LANES), 1)
    left = lane < HEAD_DIM

    def block(i, carry):
        r0 = pl.multiple_of(i * W, W)
        rows = pl.ds(r0, W)
        before = pl.ds(pl.multiple_of(jnp.maximum(r0 - W, 0), W), W)
        has_prev = jnp.logical_or(pl.program_id(1) > 0, i > 0)
        valid = jnp.logical_or(jnp.logical_and(in_prev, has_prev), in_cur)
        for kv in range(N_KV_HEADS // 2):
            kvc = slice(kv * LANES, (kv + 1) * LANES)
            k_prev = jnp.where(i > 0, kc_ref[before, kvc], kp_ref[:, kvc])
            v_prev = jnp.where(i > 0, vc_ref[before, kvc], vp_ref[:, kvc])
            kpair = jnp.concatenate([k_prev, kc_ref[rows, kvc]], axis=0)
            vpair = jnp.concatenate([v_prev, vc_ref[rows, kvc]], axis=0)
            for g in range(Q_PER_KV):
                j = kv * Q_PER_KV + g
                col = slice(j * LANES, (j + 1) * LANES)
                qc = q_ref[rows, col].astype(F32) * (HEAD_DIM ** -0.5)
                q2 = jnp.concatenate([jnp.where(left, qc, 0.0), jnp.where(left, 0.0, qc)],
                                     axis=0).astype(BF16)
                s2 = lax.dot_general(q2, kpair, (((1,), (1,)), ((), ())),
                                     preferred_element_type=F32)
                ps, rs = [], []
                for half in range(2):
                    s = jnp.where(valid, s2[half * W:(half + 1) * W, :], NEG)
                    sink = sinks_ref[2 * j + half]
                    m = jnp.maximum(jnp.max(s, axis=-1, keepdims=True), sink)
                    p = jnp.exp(s - m)
                    denom = jnp.sum(p, axis=-1, keepdims=True) + jnp.exp(sink - m)
                    ps.append(p.astype(BF16))
                    rs.append(1.0 / denom)
                o2 = jnp.dot(jnp.concatenate(ps, axis=0), vpair, preferred_element_type=F32)
                o = jnp.where(left, o2[:W, :] * rs[0], o2[W:, :] * rs[1])
                o_ref[rows, col] = o.astype(BF16)
        return carry

    lax.fori_loop(0, nsub, block, 0)


def _attention(proj, sinks_ordered, batch, seq):
    rows = min(ATTN_ROWS, seq)
    sub = rows // WINDOW
    nt = seq // rows
    T = batch * seq
    row = lambda b, n: b * nt + n
    prow = lambda b, n: (b * nt + n) * sub - jnp.minimum(n, 1)
    return pl.pallas_call(
        _attn_kernel,
        grid=(batch, nt),
        in_specs=[
            pl.BlockSpec(memory_space=pltpu.SMEM),
            pl.BlockSpec((rows, Q_DIM), lambda b, n: (row(b, n), OFF_Q // Q_DIM)),
            pl.BlockSpec((WINDOW, KV_DIM), lambda b, n: (prow(b, n), OFF_K // KV_DIM)),
            pl.BlockSpec((rows, KV_DIM), lambda b, n: (row(b, n), OFF_K // KV_DIM)),
            pl.BlockSpec((WINDOW, KV_DIM), lambda b, n: (prow(b, n), OFF_V // KV_DIM)),
            pl.BlockSpec((rows, KV_DIM), lambda b, n: (row(b, n), OFF_V // KV_DIM)),
        ],
        out_specs=pl.BlockSpec((rows, Q_DIM), lambda b, n: (row(b, n), 0)),
        out_shape=jax.ShapeDtypeStruct((T, Q_DIM), BF16),
        compiler_params=pltpu.CompilerParams(
            dimension_semantics=("arbitrary", "arbitrary"), vmem_limit_bytes=VMEM_LIMIT_BYTES),
        name="swa_attention",
    )(sinks_ordered, proj, proj, proj, proj, proj)


def _head_expand_matrix():
    k = jnp.arange(LANES)[:, None]
    c = jnp.arange(D_INNER)[None, :]
    return jnp.logical_and(k < 3 * N_SSD_HEADS, k % N_SSD_HEADS == c // SSD_HEAD_DIM).astype(BF16)


def _split3_lanes(v, lane):
    hi = v.astype(BF16).astype(F32)
    r1 = v - hi
    mid = r1.astype(BF16).astype(F32)
    parts = jnp.where(lane < N_SSD_HEADS, hi, jnp.where(lane < 2 * N_SSD_HEADS, mid, r1 - mid))
    return parts.astype(BF16)


def _ssd_kernel(z_ref, xs_ref, b_ref, c_ref, dt_ref, dtT_ref,
                alog_ref, alogT_ref, dskip_ref, nw_ref, emat_ref,
                y_ref, state_ref):
    L = CHUNK

    @pl.when(pl.program_id(1) == 0)
    def _():
        state_ref[...] = jnp.zeros(state_ref.shape, F32)

    def chunk(i, carry):
        r0 = pl.multiple_of(i * L, L)
        rows = pl.ds(r0, L)
        dt = dt_ref[rows, :]
        dtT = dtT_ref[i]
        dA = dt * (-jnp.exp(alog_ref[...]))
        dAT = dtT * (-jnp.exp(alogT_ref[...]))

        ri = lax.broadcasted_iota(jnp.int32, (L, L), 0)
        ci = lax.broadcasted_iota(jnp.int32, (L, L), 1)
        causal = ci <= ri
        tri = jnp.where(causal, 1.0, 0.0).astype(BF16)
        triT = jnp.where(ri <= ci, 1.0, 0.0).astype(BF16)
        a_cs = sum(jnp.dot(tri, p, preferred_element_type=F32) for p in _split3(dA))
        a_csT = sum(jnp.dot(p, triT, preferred_element_type=F32) for p in _split3(dAT))

        a_last = a_cs[L - 1:L, :]
        ea = jnp.exp(a_cs)
        w_state = dt * jnp.exp(a_last - a_cs)
        a2 = a_cs * LOG2E
        a2T = (a_csT - jnp.log(dtT)) * LOG2E

        lane = lax.broadcasted_iota(jnp.int32, (L, LANES), 1)
        lane_lt_half = lane < SSD_HEAD_DIM
        ea_parts = _split3_lanes(ea, lane)
        ws_parts = _split3_lanes(w_state, lane)

        for g in range(N_SSD_GROUPS):
            gcol = slice(g * GROUP_W, (g + 1) * GROUP_W)
            ncol = slice(g * D_STATE, (g + 1) * D_STATE)
            xs_bf = xs_ref[rows, gcol]
            xs = xs_bf.astype(F32)
            bg_bf = b_ref[rows, ncol]
            cg_bf = c_ref[rows, ncol]
            cb = lax.dot_general(cg_bf, bg_bf, (((1,), (1,)), ((), ())), preferred_element_type=F32)
            ea_g = jnp.dot(ea_parts, emat_ref[:, gcol], preferred_element_type=F32)
            ws_g = jnp.dot(ws_parts, emat_ref[:, gcol], preferred_element_type=F32)
            prev = state_ref[g]
            y_off = jnp.dot(cg_bf, prev.astype(BF16), preferred_element_type=F32) * ea_g
            xw = (xs * ws_g).astype(BF16)
            bgT = bg_bf.astype(F32).T.astype(BF16)
            new_states = jnp.dot(bgT, xw, preferred_element_type=F32)
            state_ref[g] = prev * ea_g[L - 1:L, :] + new_states
            y_pairs = []
            for jp in range(HEADS_PER_GROUP // 2):
                h0 = g * HEADS_PER_GROUP + 2 * jp
                xs_pair_bf = xs_bf[:, jp * LANES:(jp + 1) * LANES]
                yd = []
                for h in (h0, h0 + 1):
                    seg2 = a2[:, h:h + 1] - a2T[h:h + 1, :]
                    m = cb * jnp.exp2(jnp.where(causal, seg2, NEG))
                    yd.append(jnp.dot(m.astype(BF16), xs_pair_bf, preferred_element_type=F32))
                y_pairs.append(jnp.where(lane_lt_half, yd[0], yd[1])
                               + y_off[:, jp * LANES:(jp + 1) * LANES])
            y = jnp.concatenate(y_pairs, axis=-1) + xs * dskip_ref[:, gcol]
            y = y * _silu_of_half(z_ref[rows, gcol].astype(F32))
            ms = jnp.mean(y * y, axis=-1, keepdims=True)
            y_ref[rows, gcol] = (y * lax.rsqrt(ms + EPS) * nw_ref[:, gcol]).astype(BF16)
        return carry

    lax.fori_loop(0, z_ref.shape[0] // L, chunk, 0)


def _ssd(proj, dt, dtT, a_log, d_skip, norm_w, batch, seq):
    sub = min(SSD_SUB, seq // CHUNK)
    rows = sub * CHUNK
    nc = seq // rows
    T = batch * seq
    row = lambda b, n: b * nc + n
    alog_row = _tile3_heads(a_log.reshape(1, N_SSD_HEADS))
    alog_col = a_log.reshape(N_SSD_HEADS, 1)
    dskip_row = jnp.repeat(d_skip, SSD_HEAD_DIM).reshape(1, D_INNER)
    emat = _head_expand_matrix()
    return pl.pallas_call(
        _ssd_kernel,
        grid=(batch, nc),
        in_specs=[
            pl.BlockSpec((rows, D_INNER), lambda b, n: (row(b, n), OFF_Z // D_INNER)),
            pl.BlockSpec((rows, D_INNER), lambda b, n: (row(b, n), OFF_XS // D_INNER)),
            pl.BlockSpec((rows, BC_DIM), lambda b, n: (row(b, n), OFF_B // BC_DIM)),
            pl.BlockSpec((rows, BC_DIM), lambda b, n: (row(b, n), OFF_C // BC_DIM)),
            pl.BlockSpec((rows, LANES), lambda b, n: (row(b, n), 0)),
            pl.BlockSpec((sub, N_SSD_HEADS, CHUNK), lambda b, n: (row(b, n), 0, 0)),
            _resident((1, LANES)), _resident((N_SSD_HEADS, 1)),
            _resident((1, D_INNER)), _resident((1, D_INNER)),
            _resident(emat.shape),
        ],
        out_specs=pl.BlockSpec((rows, D_INNER), lambda b, n: (row(b, n), 0)),
        out_shape=jax.ShapeDtypeStruct((T, D_INNER), BF16),
        scratch_shapes=[pltpu.VMEM((N_SSD_GROUPS, D_STATE, GROUP_W), F32)],
        compiler_params=pltpu.CompilerParams(
            dimension_semantics=("arbitrary", "arbitrary"), vmem_limit_bytes=VMEM_LIMIT_BYTES),
        name="ssd_mixer",
    )(proj, proj, proj, proj, dt, dtT, alog_row, alog_col, dskip_row, norm_w, emat)


def _merge_kernel(x_ref, attn_ref, y_ref, ga_ref, gs_ref, bg_ref, wa_ref, ws_ref, wo_ref, h_ref):
    attn = jnp.dot(attn_ref[...], wa_ref[...], preferred_element_type=F32)
    ssd = jnp.dot(y_ref[...], ws_ref[...], preferred_element_type=F32)
    gate_a = _sigmoid(ga_ref[...].astype(F32) + bg_ref[:, :D_MODEL])
    gate_s = _sigmoid(gs_ref[...].astype(F32) + bg_ref[:, D_MODEL:])
    mixed = (gate_a * attn + gate_s * ssd).astype(BF16)
    h_ref[...] = x_ref[...] + jnp.dot(mixed, wo_ref[...], preferred_element_type=F32)


def _merge(x2d, attn, y, proj, b_gate, w_attn_o, w_ssd_o, w_out):
    T = x2d.shape[0]
    tm = min(MERGE_TM, T)
    return pl.pallas_call(
        _merge_kernel,
        grid=(T // tm,),
        in_specs=[
            pl.BlockSpec((tm, D_MODEL), lambda i: (i, 0)),
            pl.BlockSpec((tm, Q_DIM), lambda i: (i, 0)),
            pl.BlockSpec((tm, D_INNER), lambda i: (i, 0)),
            pl.BlockSpec((tm, D_MODEL), lambda i: (i, OFF_GA // D_MODEL)),
            pl.BlockSpec((tm, D_MODEL), lambda i: (i, OFF_GS // D_MODEL)),
            _resident((1, 2 * D_MODEL)),
            _resident((Q_DIM, D_MODEL)), _resident((D_INNER, D_MODEL)), _resident((D_MODEL, D_MODEL)),
        ],
        out_specs=pl.BlockSpec((tm, D_MODEL), lambda i: (i, 0)),
        out_shape=jax.ShapeDtypeStruct((T, D_MODEL), F32),
        compiler_params=pltpu.CompilerParams(
            dimension_semantics=("arbitrary",), vmem_limit_bytes=VMEM_LIMIT_BYTES),
        name="gated_merge",
    )(x2d, attn, y, proj, proj, b_gate, w_attn_o, w_ssd_o, w_out)


def _ffn_kernel(h_ref, n2_ref, wup_ref, cw_ref, cb_ref, wdn_ref, fn_ref, o_ref,
                buf_ref, tail_ref, act_ref):
    h = h_ref[...]
    ms = jnp.mean(h * h, axis=-1, keepdims=True)
    hn = (h * lax.rsqrt(ms + EPS) * n2_ref[...]).astype(BF16)

    @pl.when(pl.program_id(1) == 0)
    def _():
        tail_ref[...] = jnp.zeros(tail_ref.shape, F32)

    for c in range(FFN_NCHUNK):
        vs = slice(c * FFN_CW, (c + 1) * FFN_CW)
        gs = slice(D_FF + c * FFN_CW, D_FF + (c + 1) * FFN_CW)
        u = jnp.concatenate([jnp.dot(hn, wup_ref[:, vs], preferred_element_type=F32),
                             jnp.dot(hn, wup_ref[:, gs], preferred_element_type=F32)], axis=-1)
        w = jnp.concatenate([cw_ref[:, vs], cw_ref[:, gs]], axis=-1)
        b = jnp.concatenate([cb_ref[:, vs], cb_ref[:, gs]], axis=-1)
        acc = _conv_rows(u, buf_ref, tail_ref.at[c], w, b, FFN_CONV)
        val = acc[:, :FFN_CW]
        half_gate = acc[:, FFN_CW:]
        act_ref[:, vs] = (_silu_of_half(half_gate) * val).astype(BF16)

    h2 = h + jnp.dot(act_ref[...], wdn_ref[...], preferred_element_type=F32)
    ms2 = jnp.mean(h2 * h2, axis=-1, keepdims=True)
    o_ref[...] = h2 * lax.rsqrt(ms2 + EPS) * fn_ref[...]


def _ffn(h2d, norm2_w, w_up, conv_w, conv_b, w_down, final_w, batch, seq):
    tm = min(FFN_TM, seq)
    nt = seq // tm
    T = batch * seq
    return pl.pallas_call(
        _ffn_kernel,
        grid=(batch, nt),
        in_specs=[
            pl.BlockSpec((tm, D_MODEL), lambda b, n: (b * nt + n, 0)),
            _resident((1, D_MODEL)),
            _resident((D_MODEL, 2 * D_FF)),
            _resident((FFN_CONV, 2 * D_FF)),
            _resident((1, 2 * D_FF)),
            _resident((D_FF, D_MODEL)),
            _resident((1, D_MODEL)),
        ],
        out_specs=pl.BlockSpec((tm, D_MODEL), lambda b, n: (b * nt + n, 0)),
        out_shape=jax.ShapeDtypeStruct((T, D_MODEL), F32),
        scratch_shapes=[
            pltpu.VMEM((SUBLANES + tm, 2 * FFN_CW), F32),
            pltpu.VMEM((FFN_NCHUNK, SUBLANES, 2 * FFN_CW), F32),
            pltpu.VMEM((tm, D_FF), BF16),
        ],
        compiler_params=pltpu.CompilerParams(
            dimension_semantics=("arbitrary", "arbitrary"), vmem_limit_bytes=VMEM_LIMIT_BYTES),
        name="conv_ffn",
    )(h2d, norm2_w, w_up, conv_w, conv_b, w_down, final_w)


def _scale_gate_half(t):
    return jnp.concatenate([t[..., :D_FF], 0.5 * t[..., D_FF:]], axis=-1)


def kernel(x, norm1_w, w_in, b_gate, attn_sinks, w_attn_o, ssd_conv_w, ssd_conv_b, dt_bias, a_log,
           d_skip, ssd_norm_w, w_ssd_o, w_out, norm2_w, w_up, ffn_conv_w, ffn_conv_b, w_down,
           final_norm_w):
    batch, seq, _ = x.shape
    T = batch * seq
    assert norm1_w.shape[0] == 1, "single-layer kernel"
    assert seq % ATTN_ROWS == 0 and seq % (SSD_SUB * CHUNK) == 0 and seq % IN_TM == 0

    w = w_in[0]
    o_q, o_k, o_v, o_z = 0, Q_DIM, Q_DIM + KV_DIM, Q_DIM + 2 * KV_DIM
    o_xs = o_z + D_INNER
    o_b = o_xs + D_INNER
    o_c = o_b + BC_DIM
    o_dt = o_c + BC_DIM
    o_ga = o_dt + N_SSD_HEADS
    o_gs = o_ga + D_MODEL
    cols = lambda a, n: w[:, a:a + n]
    w_q = jnp.concatenate([cols(o_q + hd * HEAD_DIM, HEAD_DIM) for hd in ATTN_HEAD_ORDER], axis=1)
    w_ao = w_attn_o[0].reshape(N_Q_HEADS, HEAD_DIM, D_MODEL)
    w_ao = jnp.concatenate([w_ao[hd] for hd in ATTN_HEAD_ORDER], axis=0).astype(BF16)
    sinks = jnp.stack([attn_sinks[0][hd] for hd in ATTN_HEAD_ORDER])
    w_main = jnp.concatenate([
        0.5 * cols(o_z, D_INNER), cols(o_xs, D_INNER), w_q, cols(o_ga, D_MODEL),
        cols(o_gs, D_MODEL), cols(o_b, BC_DIM), cols(o_c, BC_DIM), cols(o_k, KV_DIM),
        cols(o_v, KV_DIM)], axis=1).astype(BF16)
    w_dt = cols(o_dt, N_SSD_HEADS)
    w_dt_pad = _tile3_heads(w_dt).astype(BF16)
    w_dtT = w_dt.T.astype(BF16)
    dtb_row = _tile3_heads(dt_bias[0].reshape(1, N_SSD_HEADS))
    dtb_col = dt_bias[0].reshape(N_SSD_HEADS, 1)

    x2d = x.reshape(T, D_MODEL)
    proj, dt, dtT = _inproj(x2d, norm1_w[0].reshape(1, D_MODEL), w_main, 0.5 * ssd_conv_w[0],
                            0.5 * ssd_conv_b[0].reshape(1, XBC_DIM), w_dt_pad, w_dtT, dtb_row, dtb_col, seq)
    attn = _attention(proj, sinks, batch, seq)
    y = _ssd(proj, dt, dtT, a_log[0], d_skip[0], ssd_norm_w[0].reshape(1, D_INNER), batch, seq)
    h = _merge(x2d, attn, y, proj, b_gate[0].reshape(1, 2 * D_MODEL), w_ao,
               w_ssd_o[0].astype(BF16), w_out[0].astype(BF16))
    out = _ffn(h, norm2_w[0].reshape(1, D_MODEL), w_up[0].astype(BF16),
               _scale_gate_half(ffn_conv_w[0]), _scale_gate_half(ffn_conv_b[0].reshape(1, 2 * D_FF)),
               w_down[0].astype(BF16), final_norm_w.reshape(1, D_MODEL), batch, seq)
    return out.reshape(batch, seq, D_MODEL)
```

```python
import functools

import jax
import jax.numpy as jnp
from jax import lax
from jax.experimental import pallas as pl
from jax.experimental.pallas import tpu as pltpu

F32 = jnp.float32
BF16 = jnp.bfloat16

D_MODEL = 1024
N_Q_HEADS = 16
N_KV_HEADS = 4
Q_PER_KV = N_Q_HEADS // N_KV_HEADS
HEAD_DIM = 64
WINDOW = 128
D_INNER = 2048
SSD_HEAD_DIM = 64
N_SSD_HEADS = 32
N_SSD_GROUPS = 4
HEADS_PER_GROUP = N_SSD_HEADS // N_SSD_GROUPS
D_STATE = 128
SSD_CONV = 4
CHUNK = 128
D_FF = 2816
FFN_CONV = 3
EPS = 1e-5
NEG = -1e30
LOG2E = 1.4426950408889634
Q_DIM = N_Q_HEADS * HEAD_DIM
KV_DIM = N_KV_HEADS * HEAD_DIM
BC_DIM = N_SSD_GROUPS * D_STATE
XBC_DIM = D_INNER + 2 * BC_DIM
GROUP_W = D_INNER // N_SSD_GROUPS

LANES = 128
SUBLANES = 8
VMEM_LIMIT_BYTES = 56 * 1024 * 1024

OFF_Z = 0
OFF_XS = OFF_Z + D_INNER
OFF_Q = OFF_XS + D_INNER
OFF_GA = OFF_Q + Q_DIM
OFF_GS = OFF_GA + D_MODEL
OFF_B = OFF_GS + D_MODEL
OFF_C = OFF_B + BC_DIM
OFF_K = OFF_C + BC_DIM
OFF_V = OFF_K + KV_DIM
PROJ_W = OFF_V + KV_DIM

IN_TM = 512
IN_TN = 512
CONV_CHUNKS = {OFF_XS // IN_TN + k: k for k in range(D_INNER // IN_TN)}
CONV_CHUNKS[OFF_B // IN_TN] = D_INNER // IN_TN
CONV_CHUNKS[OFF_C // IN_TN] = D_INNER // IN_TN + 1
ATTN_ROWS = 1024
SSD_SUB = 8
MERGE_TM = 1024
FFN_TM = 1024
FFN_CW = 256
FFN_NCHUNK = D_FF // FFN_CW


def _silu_of_half(h):
    return h + h * jnp.tanh(h)


def _sigmoid(x):
    return 1.0 / (1.0 + jnp.exp(-x))


def _softplus(x):
    return jnp.maximum(x, 0.0) + jnp.log(1.0 + jnp.exp(-jnp.abs(x)))


def _tile3_heads(t):
    pad = jnp.zeros(t.shape[:-1] + (LANES - 3 * N_SSD_HEADS,), t.dtype)
    return jnp.concatenate([t, t, t, pad], axis=-1)


def _split3(x):
    hi = x.astype(BF16)
    r1 = x - hi.astype(F32)
    mid = r1.astype(BF16)
    lo = (r1 - mid.astype(F32)).astype(BF16)
    return hi, mid, lo


def _resident(shape):
    return pl.BlockSpec(shape, lambda *_: (0,) * len(shape), pipeline_mode=pl.Buffered(1))


def _conv_rows(u, buf_ref, tail_ref, w, b, taps):
    tm = u.shape[0]
    buf_ref[0:SUBLANES, :] = tail_ref[...]
    buf_ref[SUBLANES:SUBLANES + tm, :] = u
    tail_ref[...] = u[tm - SUBLANES:tm, :]
    acc = u * w[taps - 1:taps, :] + b
    for k in range(taps - 1):
        off = SUBLANES - (taps - 1) + k
        acc = acc + buf_ref[off:off + tm, :] * w[k:k + 1, :]
    return acc


def _inproj_kernel(x_ref, nw_ref, w_ref, cw_ref, cb_ref, wdt_ref, wdtT_ref, dtb_ref, dtbT_ref,
                   proj_ref, dt_ref, dtT_ref, buf_ref, tail_ref, *, tiles_per_seq):
    @pl.when(pl.program_id(0) % tiles_per_seq == 0)
    def _():
        tail_ref[...] = jnp.zeros(tail_ref.shape, F32)

    x = x_ref[...]
    ms = jnp.mean(x * x, axis=-1, keepdims=True)
    xn = (x * lax.rsqrt(ms + EPS) * nw_ref[...]).astype(BF16)
    for c in range(PROJ_W // IN_TN):
        cs = slice(c * IN_TN, (c + 1) * IN_TN)
        u = jnp.dot(xn, w_ref[:, cs], preferred_element_type=F32)
        if c in CONV_CHUNKS:
            k = CONV_CHUNKS[c]
            ks = slice(k * IN_TN, (k + 1) * IN_TN)
            u = _silu_of_half(_conv_rows(u, buf_ref, tail_ref.at[k], cw_ref[:, ks], cb_ref[:, ks], SSD_CONV))
        proj_ref[:, cs] = u.astype(BF16)
    dt_raw = jnp.dot(xn, wdt_ref[...], preferred_element_type=F32)
    dt_ref[...] = _softplus(dt_raw + dtb_ref[...])
    dtT_raw = lax.dot_general(wdtT_ref[...], xn, (((1,), (1,)), ((), ())),
                              preferred_element_type=F32)
    dtT = _softplus(dtT_raw + dtbT_ref[...])
    for c in range(dtT_ref.shape[0]):
        dtT_ref[c] = dtT[:, c * CHUNK:(c + 1) * CHUNK]


def _inproj(x2d, norm_w, w_main, conv_w_half, conv_b_half, w_dt, w_dtT, dt_bias_row, dt_bias_col, seq):
    T = x2d.shape[0]
    tm = min(IN_TM, seq)
    return pl.pallas_call(
        functools.partial(_inproj_kernel, tiles_per_seq=seq // tm),
        grid=(T // tm,),
        in_specs=[
            pl.BlockSpec((tm, D_MODEL), lambda i: (i, 0)),
            _resident((1, D_MODEL)),
            _resident((D_MODEL, PROJ_W)),
            _resident((SSD_CONV, XBC_DIM)),
            _resident((1, XBC_DIM)),
            _resident((D_MODEL, LANES)),
            _resident((N_SSD_HEADS, D_MODEL)),
            _resident((1, LANES)),
            _resident((N_SSD_HEADS, 1)),
        ],
        out_specs=[
            pl.BlockSpec((tm, PROJ_W), lambda i: (i, 0)),
            pl.BlockSpec((tm, LANES), lambda i: (i, 0)),
            pl.BlockSpec((tm // CHUNK, N_SSD_HEADS, CHUNK), lambda i: (i, 0, 0)),
        ],
        out_shape=[
            jax.ShapeDtypeStruct((T, PROJ_W), BF16),
            jax.ShapeDtypeStruct((T, LANES), F32),
            jax.ShapeDtypeStruct((T // CHUNK, N_SSD_HEADS, CHUNK), F32),
        ],
        scratch_shapes=[
            pltpu.VMEM((SUBLANES + tm, IN_TN), F32),
            pltpu.VMEM((XBC_DIM // IN_TN, SUBLANES, IN_TN), F32),
        ],
        compiler_params=pltpu.CompilerParams(
            dimension_semantics=("arbitrary",), vmem_limit_bytes=VMEM_LIMIT_BYTES),
        name="inproj",
    )(x2d, norm_w, w_main, conv_w_half, conv_b_half, w_dt, w_dtT, dt_bias_row, dt_bias_col)


ATTN_HEAD_ORDER = tuple(
    (2 * (j // Q_PER_KV) + half) * Q_PER_KV + j % Q_PER_KV
    for j in range(N_Q_HEADS // 2) for half in range(2))


def _attn_kernel(sinks_ref, q_ref, kp_ref, kc_ref, vp_ref, vc_ref, o_ref):
    W = WINDOW
    nsub = q_ref.shape[0] // W
    qi = lax.broadcasted_iota(jnp.int32, (W, 2 * W), 0)
    si = lax.broadcasted_iota(jnp.int32, (W, 2 * W), 1)
    in_prev = jnp.logical_and(si < W, si > qi)
    in_cur = jnp.logical_and(si >= W, si - W <= qi)
    lane = lax.broadcasted_iota(jnp.int32, (W, LANES), 1)
    left = lane < HEAD_DIM

    def block(i, carry):
        r0 = pl.multiple_of(i * W, W)
        rows = pl.ds(r0, W)
        before = pl.ds(pl.multiple_of(jnp.maximum(r0 - W, 0), W), W)
        has_prev = jnp.logical_or(pl.program_id(1) > 0, i > 0)
        valid = jnp.logical_or(jnp.logical_and(in_prev, has_prev), in_cur)
        scores, vpairs = [], []
        for kv in range(N_KV_HEADS // 2):
            kvc = slice(kv * LANES, (kv + 1) * LANES)
            k_prev = jnp.where(i > 0, kc_ref[before, kvc], kp_ref[:, kvc])
            v_prev = jnp.where(i > 0, vc_ref[before, kvc], vp_ref[:, kvc])
            kpair = jnp.concatenate([k_prev, kc_ref[rows, kvc]], axis=0)
            vpairs.append(jnp.concatenate([v_prev, vc_ref[rows, kvc]], axis=0))
            for g in range(Q_PER_KV):
                j = kv * Q_PER_KV + g
                col = slice(j * LANES, (j + 1) * LANES)
                qc = q_ref[rows, col].astype(F32) * (HEAD_DIM ** -0.5)
                q2 = jnp.concatenate([jnp.where(left, qc, 0.0), jnp.where(left, 0.0, qc)],
                                     axis=0).astype(BF16)
                scores.append(lax.dot_general(q2, kpair, (((1,), (1,)), ((), ())),
                                              preferred_element_type=F32))
        for kv in range(N_KV_HEADS // 2):
            vpair = vpairs[kv]
            for g in range(Q_PER_KV):
                j = kv * Q_PER_KV + g
                col = slice(j * LANES, (j + 1) * LANES)
                s2 = scores[j]
                ps, rs = [], []
                for half in range(2):
                    s = jnp.where(valid, s2[half * W:(half + 1) * W, :], NEG)
                    sink = sinks_ref[2 * j + half]
                    m = jnp.maximum(jnp.max(s, axis=-1, keepdims=True), sink)
                    p = jnp.exp(s - m)
                    denom = jnp.sum(p, axis=-1, keepdims=True) + jnp.exp(sink - m)
                    ps.append(p.astype(BF16))
                    rs.append(1.0 / denom)
                o2 = jnp.dot(jnp.concatenate(ps, axis=0), vpair, preferred_element_type=F32)
                o = jnp.where(left, o2[:W, :] * rs[0], o2[W:, :] * rs[1])
                o_ref[rows, col] = o.astype(BF16)
        return carry

    lax.fori_loop(0, nsub, block, 0)


def _attention(proj, sinks_ordered, batch, seq):
    rows = min(ATTN_ROWS, seq)
    sub = rows // WINDOW
    nt = seq // rows
    T = batch * seq
    row = lambda b, n: b * nt + n
    prow = lambda b, n: (b * nt + n) * sub - jnp.minimum(n, 1)
    return pl.pallas_call(
        _attn_kernel,
        grid=(batch, nt),
        in_specs=[
            pl.BlockSpec(memory_space=pltpu.SMEM),
            pl.BlockSpec((rows, Q_DIM), lambda b, n: (row(b, n), OFF_Q // Q_DIM)),
            pl.BlockSpec((WINDOW, KV_DIM), lambda b, n: (prow(b, n), OFF_K // KV_DIM)),
            pl.BlockSpec((rows, KV_DIM), lambda b, n: (row(b, n), OFF_K // KV_DIM)),
            pl.BlockSpec((WINDOW, KV_DIM), lambda b, n: (prow(b, n), OFF_V // KV_DIM)),
            pl.BlockSpec((rows, KV_DIM), lambda b, n: (row(b, n), OFF_V // KV_DIM)),
        ],
        out_specs=pl.BlockSpec((rows, Q_DIM), lambda b, n: (row(b, n), 0)),
        out_shape=jax.ShapeDtypeStruct((T, Q_DIM), BF16),
        compiler_params=pltpu.CompilerParams(
            dimension_semantics=("arbitrary", "arbitrary"), vmem_limit_bytes=VMEM_LIMIT_BYTES),
        name="swa_attention",
    )(sinks_ordered, proj, proj, proj, proj, proj)


def _head_expand_matrix():
    k = jnp.arange(LANES)[:, None]
    c = jnp.arange(D_INNER)[None, :]
    return jnp.logical_and(k < 3 * N_SSD_HEADS, k % N_SSD_HEADS == c // SSD_HEAD_DIM).astype(BF16)


def _split3_lanes(v, lane):
    hi = v.astype(BF16).astype(F32)
    r1 = v - hi
    mid = r1.astype(BF16).astype(F32)
    parts = jnp.where(lane < N_SSD_HEADS, hi, jnp.where(lane < 2 * N_SSD_HEADS, mid, r1 - mid))
    return parts.astype(BF16)


def _ssd_kernel(z_ref, xs_ref, b_ref, c_ref, dt_ref, dtT_ref,
                alog_ref, alogT_ref, dskip_ref, nw_ref, emat_ref,
                y_ref, state_ref):
    L = CHUNK

    @pl.when(pl.program_id(1) == 0)
    def _():
        state_ref[...] = jnp.zeros(state_ref.shape, F32)

    def chunk(i, carry):
        r0 = pl.multiple_of(i * L, L)
        rows = pl.ds(r0, L)
        dt = dt_ref[rows, :]
        dtT = dtT_ref[i]
        dA = dt * (-jnp.exp(alog_ref[...]))
        dAT = dtT * (-jnp.exp(alogT_ref[...]))

        ri = lax.broadcasted_iota(jnp.int32, (L, L), 0)
        ci = lax.broadcasted_iota(jnp.int32, (L, L), 1)
        causal = ci <= ri
        tri = jnp.where(causal, 1.0, 0.0).astype(BF16)
        triT = jnp.where(ri <= ci, 1.0, 0.0).astype(BF16)
        a_cs = sum(jnp.dot(tri, p, preferred_element_type=F32) for p in _split3(dA))
        a_csT = sum(jnp.dot(p, triT, preferred_element_type=F32) for p in _split3(dAT))

        a_last = a_cs[L - 1:L, :]
        ea = jnp.exp(a_cs)
        w_state = dt * jnp.exp(a_last - a_cs)
        a2 = a_cs * LOG2E
        a2T = (a_csT - jnp.log(dtT)) * LOG2E

        lane = lax.broadcasted_iota(jnp.int32, (L, LANES), 1)
        lane_lt_half = lane < SSD_HEAD_DIM
        ea_parts = _split3_lanes(ea, lane)
        ws_parts = _split3_lanes(w_state, lane)

        for g in range(N_SSD_GROUPS):
            gcol = slice(g * GROUP_W, (g + 1) * GROUP_W)
            ncol = slice(g * D_STATE, (g + 1) * D_STATE)
            xs_bf = xs_ref[rows, gcol]
            xs = xs_bf.astype(F32)
            bg_bf = b_ref[rows, ncol]
            cg_bf = c_ref[rows, ncol]
            cb = lax.dot_general(cg_bf, bg_bf, (((1,), (1,)), ((), ())), preferred_element_type=F32)
            ea_g = jnp.dot(ea_parts, emat_ref[:, gcol], preferred_element_type=F32)
            ws_g = jnp.dot(ws_parts, emat_ref[:, gcol], preferred_element_type=F32)
            prev = state_ref[g]
            y_off = jnp.dot(cg_bf, prev.astype(BF16), preferred_element_type=F32) * ea_g
            xw = (xs * ws_g).astype(BF16)
            bgT = bg_bf.astype(F32).T.astype(BF16)
            new_states = jnp.dot(bgT, xw, preferred_element_type=F32)
            state_ref[g] = prev * ea_g[L - 1:L, :] + new_states
            y_pairs = []
            for jp in range(HEADS_PER_GROUP // 2):
                h0 = g * HEADS_PER_GROUP + 2 * jp
                xs_pair_bf = xs_bf[:, jp * LANES:(jp + 1) * LANES]
                yd = []
                for h in (h0, h0 + 1):
                    seg2 = a2[:, h:h + 1] - a2T[h:h + 1, :]
                    m = cb * jnp.exp2(jnp.where(causal, seg2, NEG))
                    yd.append(jnp.dot(m.astype(BF16), xs_pair_bf, preferred_element_type=F32))
                y_pairs.append(jnp.where(lane_lt_half, yd[0], yd[1])
                               + y_off[:, jp * LANES:(jp + 1) * LANES])
            y = jnp.concatenate(y_pairs, axis=-1) + xs * dskip_ref[:, gcol]
            y = y * _silu_of_half(z_ref[rows, gcol].astype(F32))
            ms = jnp.mean(y * y, axis=-1, keepdims=True)
            y_ref[rows, gcol] = (y * lax.rsqrt(ms + EPS) * nw_ref[:, gcol]).astype(BF16)
        return carry

    lax.fori_loop(0, z_ref.shape[0] // L, chunk, 0)


def _ssd(proj, dt, dtT, a_log, d_skip, norm_w, batch, seq):
    sub = min(SSD_SUB, seq // CHUNK)
    rows = sub * CHUNK
    nc = seq // rows
    T = batch * seq
    row = lambda b, n: b * nc + n
    alog_row = _tile3_heads(a_log.reshape(1, N_SSD_HEADS))
    alog_col = a_log.reshape(N_SSD_HEADS, 1)
    dskip_row = jnp.repeat(d_skip, SSD_HEAD_DIM).reshape(1, D_INNER)
    emat = _head_expand_matrix()
    return pl.pallas_call(
        _ssd_kernel,
        grid=(batch, nc),
        in_specs=[
            pl.BlockSpec((rows, D_INNER), lambda b, n: (row(b, n), OFF_Z // D_INNER)),
            pl.BlockSpec((rows, D_INNER), lambda b, n: (row(b, n), OFF_XS // D_INNER)),
            pl.BlockSpec((rows, BC_DIM), lambda b, n: (row(b, n), OFF_B // BC_DIM)),
            pl.BlockSpec((rows, BC_DIM), lambda b, n: (row(b, n), OFF_C // BC_DIM)),
            pl.BlockSpec((rows, LANES), lambda b, n: (row(b, n), 0)),
            pl.BlockSpec((sub, N_SSD_HEADS, CHUNK), lambda b, n: (row(b, n), 0, 0)),
            _resident((1, LANES)), _resident((N_SSD_HEADS, 1)),
            _resident((1, D_INNER)), _resident((1, D_INNER)),
            _resident(emat.shape),
        ],
        out_specs=pl.BlockSpec((rows, D_INNER), lambda b, n: (row(b, n), 0)),
        out_shape=jax.ShapeDtypeStruct((T, D_INNER), BF16),
        scratch_shapes=[pltpu.VMEM((N_SSD_GROUPS, D_STATE, GROUP_W), F32)],
        compiler_params=pltpu.CompilerParams(
            dimension_semantics=("arbitrary", "arbitrary"), vmem_limit_bytes=VMEM_LIMIT_BYTES),
        name="ssd_mixer",
    )(proj, proj, proj, proj, dt, dtT, alog_row, alog_col, dskip_row, norm_w, emat)


def _merge_kernel(x_ref, attn_ref, y_ref, ga_ref, gs_ref, bg_ref, wa_ref, ws_ref, wo_ref, h_ref):
    attn = jnp.dot(attn_ref[...], wa_ref[...], preferred_element_type=F32)
    ssd = jnp.dot(y_ref[...], ws_ref[...], preferred_element_type=F32)
    gate_a = _sigmoid(ga_ref[...].astype(F32) + bg_ref[:, :D_MODEL])
    gate_s = _sigmoid(gs_ref[...].astype(F32) + bg_ref[:, D_MODEL:])
    mixed = (gate_a * attn + gate_s * ssd).astype(BF16)
    h_ref[...] = x_ref[...] + jnp.dot(mixed, wo_ref[...], preferred_element_type=F32)


def _merge(x2d, attn, y, proj, b_gate, w_attn_o, w_ssd_o, w_out):
    T = x2d.shape[0]
    tm = min(MERGE_TM, T)
    return pl.pallas_call(
        _merge_kernel,
        grid=(T // tm,),
        in_specs=[
            pl.BlockSpec((tm, D_MODEL), lambda i: (i, 0)),
            pl.BlockSpec((tm, Q_DIM), lambda i: (i, 0)),
            pl.BlockSpec((tm, D_INNER), lambda i: (i, 0)),
            pl.BlockSpec((tm, D_MODEL), lambda i: (i, OFF_GA // D_MODEL)),
            pl.BlockSpec((tm, D_MODEL), lambda i: (i, OFF_GS // D_MODEL)),
            _resident((1, 2 * D_MODEL)),
            _resident((Q_DIM, D_MODEL)), _resident((D_INNER, D_MODEL)), _resident((D_MODEL, D_MODEL)),
        ],
        out_specs=pl.BlockSpec((tm, D_MODEL), lambda i: (i, 0)),
        out_shape=jax.ShapeDtypeStruct((T, D_MODEL), F32),
        compiler_params=pltpu.CompilerParams(
            dimension_semantics=("arbitrary",), vmem_limit_bytes=VMEM_LIMIT_BYTES),
        name="gated_merge",
    )(x2d, attn, y, proj, proj, b_gate, w_attn_o, w_ssd_o, w_out)


def _ffn_kernel(h_ref, n2_ref, wup_ref, cw_ref, cb_ref, wdn_ref, fn_ref, o_ref,
                buf_ref, tail_ref, act_ref):
    h = h_ref[...]
    ms = jnp.mean(h * h, axis=-1, keepdims=True)
    hn = (h * lax.rsqrt(ms + EPS) * n2_ref[...]).astype(BF16)

    @pl.when(pl.program_id(1) == 0)
    def _():
        tail_ref[...] = jnp.zeros(tail_ref.shape, F32)

    for c in range(FFN_NCHUNK):
        vs = slice(c * FFN_CW, (c + 1) * FFN_CW)
        gs = slice(D_FF + c * FFN_CW, D_FF + (c + 1) * FFN_CW)
        u = jnp.concatenate([jnp.dot(hn, wup_ref[:, vs], preferred_element_type=F32),
                             jnp.dot(hn, wup_ref[:, gs], preferred_element_type=F32)], axis=-1)
        w = jnp.concatenate([cw_ref[:, vs], cw_ref[:, gs]], axis=-1)
        b = jnp.concatenate([cb_ref[:, vs], cb_ref[:, gs]], axis=-1)
        acc = _conv_rows(u, buf_ref, tail_ref.at[c], w, b, FFN_CONV)
        val = acc[:, :FFN_CW]
        half_gate = acc[:, FFN_CW:]
        act_ref[:, vs] = (_silu_of_half(half_gate) * val).astype(BF16)

    h2 = h + jnp.dot(act_ref[...], wdn_ref[...], preferred_element_type=F32)
    ms2 = jnp.mean(h2 * h2, axis=-1, keepdims=True)
    o_ref[...] = h2 * lax.rsqrt(ms2 + EPS) * fn_ref[...]


def _ffn(h2d, norm2_w, w_up, conv_w, conv_b, w_down, final_w, batch, seq):
    tm = min(FFN_TM, seq)
    nt = seq // tm
    T = batch * seq
    return pl.pallas_call(
        _ffn_kernel,
        grid=(batch, nt),
        in_specs=[
            pl.BlockSpec((tm, D_MODEL), lambda b, n: (b * nt + n, 0)),
            _resident((1, D_MODEL)),
            _resident((D_MODEL, 2 * D_FF)),
            _resident((FFN_CONV, 2 * D_FF)),
            _resident((1, 2 * D_FF)),
            _resident((D_FF, D_MODEL)),
            _resident((1, D_MODEL)),
        ],
        out_specs=pl.BlockSpec((tm, D_MODEL), lambda b, n: (b * nt + n, 0)),
        out_shape=jax.ShapeDtypeStruct((T, D_MODEL), F32),
        scratch_shapes=[
            pltpu.VMEM((SUBLANES + tm, 2 * FFN_CW), F32),
            pltpu.VMEM((FFN_NCHUNK, SUBLANES, 2 * FFN_CW), F32),
            pltpu.VMEM((tm, D_FF), BF16),
        ],
        compiler_params=pltpu.CompilerParams(
            dimension_semantics=("arbitrary", "arbitrary"), vmem_limit_bytes=VMEM_LIMIT_BYTES),
        name="conv_ffn",
    )(h2d, norm2_w, w_up, conv_w, conv_b, w_down, final_w)


def _scale_gate_half(t):
    return jnp.concatenate([t[..., :D_FF], 0.5 * t[..., D_FF:]], axis=-1)


def kernel(x, norm1_w, w_in, b_gate, attn_sinks, w_attn_o, ssd_conv_w, ssd_conv_b, dt_bias, a_log,
           d_skip, ssd_norm_w, w_ssd_o, w_out, norm2_w, w_up, ffn_conv_w, ffn_conv_b, w_down,
           final_norm_w):
    batch, seq, _ = x.shape
    T = batch * seq
    assert norm1_w.shape[0] == 1, "single-layer kernel"
    assert seq % ATTN_ROWS == 0 and seq % (SSD_SUB * CHUNK) == 0 and seq % IN_TM == 0

    w = w_in[0]
    o_q, o_k, o_v, o_z = 0, Q_DIM, Q_DIM + KV_DIM, Q_DIM + 2 * KV_DIM
    o_xs = o_z + D_INNER
    o_b = o_xs + D_INNER
    o_c = o_b + BC_DIM
    o_dt = o_c + BC_DIM
    o_ga = o_dt + N_SSD_HEADS
    o_gs = o_ga + D_MODEL
    cols = lambda a, n: w[:, a:a + n]
    w_q = jnp.concatenate([cols(o_q + hd * HEAD_DIM, HEAD_DIM) for hd in ATTN_HEAD_ORDER], axis=1)
    w_ao = w_attn_o[0].reshape(N_Q_HEADS, HEAD_DIM, D_MODEL)
    w_ao = jnp.concatenate([w_ao[hd] for hd in ATTN_HEAD_ORDER], axis=0).astype(BF16)
    sinks = jnp.stack([attn_sinks[0][hd] for hd in ATTN_HEAD_ORDER])
    w_main = jnp.concatenate([
        0.5 * cols(o_z, D_INNER), cols(o_xs, D_INNER), w_q, cols(o_ga, D_MODEL),
        cols(o_gs, D_MODEL), cols(o_b, BC_DIM), cols(o_c, BC_DIM), cols(o_k, KV_DIM),
        cols(o_v, KV_DIM)], axis=1).astype(BF16)
    w_dt = cols(o_dt, N_SSD_HEADS)
    w_dt_pad = _tile3_heads(w_dt).astype(BF16)
    w_dtT = w_dt.T.astype(BF16)
    dtb_row = _tile3_heads(dt_bias[0].reshape(1, N_SSD_HEADS))
    dtb_col = dt_bias[0].reshape(N_SSD_HEADS, 1)

    x2d = x.reshape(T, D_MODEL)
    proj, dt, dtT = _inproj(x2d, norm1_w[0].reshape(1, D_MODEL), w_main, 0.5 * ssd_conv_w[0],
                            0.5 * ssd_conv_b[0].reshape(1, XBC_DIM), w_dt_pad, w_dtT, dtb_row, dtb_col, seq)
    attn = _attention(proj, sinks, batch, seq)
    y = _ssd(proj, dt, dtT, a_log[0], d_skip[0], ssd_norm_w[0].reshape(1, D_INNER), batch, seq)
    h = _merge(x2d, attn, y, proj, b_gate[0].reshape(1, 2 * D_MODEL), w_ao,
               w_ssd_o[0].astype(BF16), w_out[0].astype(BF16))
    out = _ffn(h, norm2_w[0].reshape(1, D_MODEL), w_up[0].astype(BF16),
               _scale_gate_half(ffn_conv_w[0]), _scale_gate_half(ffn_conv_b[0].reshape(1, 2 * D_FF)),
               w_down[0].astype(BF16), final_norm_w.reshape(1, D_MODEL), batch, seq)
    return out.reshape(batch, seq, D_MODEL)
```

```python
import functools

import jax
import jax.numpy as jnp
from jax import lax
from jax.experimental import pallas as pl
from jax.experimental.pallas import tpu as pltpu

F32 = jnp.float32
BF16 = jnp.bfloat16

D_MODEL = 1024
N_Q_HEADS = 16
N_KV_HEADS = 4
Q_PER_KV = N_Q_HEADS // N_KV_HEADS
HEAD_DIM = 64
WINDOW = 128
D_INNER = 2048
SSD_HEAD_DIM = 64
N_SSD_HEADS = 32
N_SSD_GROUPS = 4
HEADS_PER_GROUP = N_SSD_HEADS // N_SSD_GROUPS
D_STATE = 128
SSD_CONV = 4
CHUNK = 128
D_FF = 2816
FFN_CONV = 3
EPS = 1e-5
NEG = -1e30
LOG2E = 1.4426950408889634
Q_DIM = N_Q_HEADS * HEAD_DIM
KV_DIM = N_KV_HEADS * HEAD_DIM
BC_DIM = N_SSD_GROUPS * D_STATE
XBC_DIM = D_INNER + 2 * BC_DIM
GROUP_W = D_INNER // N_SSD_GROUPS

LANES = 128
SUBLANES = 8
VMEM_LIMIT_BYTES = 56 * 1024 * 1024

OFF_Z = 0
OFF_XS = OFF_Z + D_INNER
OFF_Q = OFF_XS + D_INNER
OFF_GA = OFF_Q + Q_DIM
OFF_GS = OFF_GA + D_MODEL
OFF_B = OFF_GS + D_MODEL
OFF_C = OFF_B + BC_DIM
OFF_K = OFF_C + BC_DIM
OFF_V = OFF_K + KV_DIM
PROJ_W = OFF_V + KV_DIM

IN_TM = 512
IN_TN = 512
CONV_CHUNKS = {OFF_XS // IN_TN + k: k for k in range(D_INNER // IN_TN)}
CONV_CHUNKS[OFF_B // IN_TN] = D_INNER // IN_TN
CONV_CHUNKS[OFF_C // IN_TN] = D_INNER // IN_TN + 1
ATTN_ROWS = 1024
SSD_SUB = 8
MERGE_TM = 512
FFN_TM = 1024
FFN_CW = 256
FFN_NCHUNK = D_FF // FFN_CW


def _silu_of_half(h):
    return h + h * jnp.tanh(h)


def _sigmoid(x):
    return 1.0 / (1.0 + jnp.exp(-x))


def _softplus(x):
    return jnp.maximum(x, 0.0) + jnp.log(1.0 + jnp.exp(-jnp.abs(x)))


def _tile3_heads(t):
    pad = jnp.zeros(t.shape[:-1] + (LANES - 3 * N_SSD_HEADS,), t.dtype)
    return jnp.concatenate([t, t, t, pad], axis=-1)


def _split3(x):
    hi = x.astype(BF16)
    r1 = x - hi.astype(F32)
    mid = r1.astype(BF16)
    lo = (r1 - mid.astype(F32)).astype(BF16)
    return hi, mid, lo


def _resident(shape):
    return pl.BlockSpec(shape, lambda *_: (0,) * len(shape), pipeline_mode=pl.Buffered(1))


def _conv_rows(u, buf_ref, tail_ref, w, b, taps):
    tm = u.shape[0]
    buf_ref[0:SUBLANES, :] = tail_ref[...]
    buf_ref[SUBLANES:SUBLANES + tm, :] = u
    tail_ref[...] = u[tm - SUBLANES:tm, :]
    acc = u * w[taps - 1:taps, :] + b
    for k in range(taps - 1):
        off = SUBLANES - (taps - 1) + k
        acc = acc + buf_ref[off:off + tm, :] * w[k:k + 1, :]
    return acc


def _inproj_kernel(x_ref, nw_ref, w_ref, cw_ref, cb_ref, wdt_ref, wdtT_ref, dtb_ref, dtbT_ref,
                   proj_ref, dt_ref, dtT_ref, buf_ref, tail_ref, *, tiles_per_seq):
    @pl.when(pl.program_id(0) % tiles_per_seq == 0)
    def _():
        tail_ref[...] = jnp.zeros(tail_ref.shape, F32)

    x = x_ref[...]
    ms = jnp.mean(x * x, axis=-1, keepdims=True)
    xn = (x * lax.rsqrt(ms + EPS) * nw_ref[...]).astype(BF16)
    for c in range(PROJ_W // IN_TN):
        cs = slice(c * IN_TN, (c + 1) * IN_TN)
        u = jnp.dot(xn, w_ref[:, cs], preferred_element_type=F32)
        if c in CONV_CHUNKS:
            k = CONV_CHUNKS[c]
            ks = slice(k * IN_TN, (k + 1) * IN_TN)
            u = _silu_of_half(_conv_rows(u, buf_ref, tail_ref.at[k], cw_ref[:, ks], cb_ref[:, ks], SSD_CONV))
        proj_ref[:, cs] = u.astype(BF16)
    dt_raw = jnp.dot(xn, wdt_ref[...], preferred_element_type=F32)
    dt_ref[...] = _softplus(dt_raw + dtb_ref[...])
    dtT_raw = lax.dot_general(wdtT_ref[...], xn, (((1,), (1,)), ((), ())),
                              preferred_element_type=F32)
    dtT = _softplus(dtT_raw + dtbT_ref[...])
    for c in range(dtT_ref.shape[0]):
        dtT_ref[c] = dtT[:, c * CHUNK:(c + 1) * CHUNK]


def _inproj(x2d, norm_w, w_main, conv_w_half, conv_b_half, w_dt, w_dtT, dt_bias_row, dt_bias_col, seq):
    T = x2d.shape[0]
    tm = min(IN_TM, seq)
    return pl.pallas_call(
        functools.partial(_inproj_kernel, tiles_per_seq=seq // tm),
        grid=(T // tm,),
        in_specs=[
            pl.BlockSpec((tm, D_MODEL), lambda i: (i, 0)),
            _resident((1, D_MODEL)),
            _resident((D_MODEL, PROJ_W)),
            _resident((SSD_CONV, XBC_DIM)),
            _resident((1, XBC_DIM)),
            _resident((D_MODEL, LANES)),
            _resident((N_SSD_HEADS, D_MODEL)),
            _resident((1, LANES)),
            _resident((N_SSD_HEADS, 1)),
        ],
        out_specs=[
            pl.BlockSpec((tm, PROJ_W), lambda i: (i, 0)),
            pl.BlockSpec((tm, LANES), lambda i: (i, 0)),
            pl.BlockSpec((tm // CHUNK, N_SSD_HEADS, CHUNK), lambda i: (i, 0, 0)),
        ],
        out_shape=[
            jax.ShapeDtypeStruct((T, PROJ_W), BF16),
            jax.ShapeDtypeStruct((T, LANES), F32),
            jax.ShapeDtypeStruct((T // CHUNK, N_SSD_HEADS, CHUNK), F32),
        ],
        scratch_shapes=[
            pltpu.VMEM((SUBLANES + tm, IN_TN), F32),
            pltpu.VMEM((XBC_DIM // IN_TN, SUBLANES, IN_TN), F32),
        ],
        compiler_params=pltpu.CompilerParams(
            dimension_semantics=("arbitrary",), vmem_limit_bytes=VMEM_LIMIT_BYTES),
        name="inproj",
    )(x2d, norm_w, w_main, conv_w_half, conv_b_half, w_dt, w_dtT, dt_bias_row, dt_bias_col)


ATTN_HEAD_ORDER = tuple(
    (2 * (j // Q_PER_KV) + half) * Q_PER_KV + j % Q_PER_KV
    for j in range(N_Q_HEADS // 2) for half in range(2))


def _attn_kernel(sinks_ref, q_ref, kp_ref, kc_ref, vp_ref, vc_ref, o_ref):
    W = WINDOW
    nsub = q_ref.shape[0] // W
    qi = lax.broadcasted_iota(jnp.int32, (W, 2 * W), 0)
    si = lax.broadcasted_iota(jnp.int32, (W, 2 * W), 1)
    in_prev = jnp.logical_and(si < W, si > qi)
    in_cur = jnp.logical_and(si >= W, si - W <= qi)
    lane = lax.broadcasted_iota(jnp.int32, (W, LANES), 1)
    left = lane < HEAD_DIM

    def block(i, carry):
        r0 = pl.multiple_of(i * W, W)
        rows = pl.ds(r0, W)
        before = pl.ds(pl.multiple_of(jnp.maximum(r0 - W, 0), W), W)
        has_prev = jnp.logical_or(pl.program_id(1) > 0, i > 0)
        valid = jnp.logical_or(jnp.logical_and(in_prev, has_prev), in_cur)
        scores, vpairs = [], []
        for kv in range(N_KV_HEADS // 2):
            kvc = slice(kv * LANES, (kv + 1) * LANES)
            k_prev = jnp.where(i > 0, kc_ref[before, kvc], kp_ref[:, kvc])
            v_prev = jnp.where(i > 0, vc_ref[before, kvc], vp_ref[:, kvc])
            kpair = jnp.concatenate([k_prev, kc_ref[rows, kvc]], axis=0)
            vpairs.append(jnp.concatenate([v_prev, vc_ref[rows, kvc]], axis=0))
            for g in range(Q_PER_KV):
                j = kv * Q_PER_KV + g
                col = slice(j * LANES, (j + 1) * LANES)
                qc = q_ref[rows, col].astype(F32) * (HEAD_DIM ** -0.5)
                q2 = jnp.concatenate([jnp.where(left, qc, 0.0), jnp.where(left, 0.0, qc)],
                                     axis=0).astype(BF16)
                scores.append(lax.dot_general(q2, kpair, (((1,), (1,)), ((), ())),
                                              preferred_element_type=F32))
        for kv in range(N_KV_HEADS // 2):
            vpair = vpairs[kv]
            for g in range(Q_PER_KV):
                j = kv * Q_PER_KV + g
                col = slice(j * LANES, (j + 1) * LANES)
                s2 = scores[j]
                ps, rs = [], []
                for half in range(2):
                    s = jnp.where(valid, s2[half * W:(half + 1) * W, :], NEG)
                    sink = sinks_ref[2 * j + half]
                    m = jnp.maximum(jnp.max(s, axis=-1, keepdims=True), sink)
                    p = jnp.exp(s - m)
                    denom = jnp.sum(p, axis=-1, keepdims=True) + jnp.exp(sink - m)
                    ps.append(p.astype(BF16))
                    rs.append(1.0 / denom)
                o2 = jnp.dot(jnp.concatenate(ps, axis=0), vpair, preferred_element_type=F32)
                o = jnp.where(left, o2[:W, :] * rs[0], o2[W:, :] * rs[1])
                o_ref[rows, col] = o.astype(BF16)
        return carry

    lax.fori_loop(0, nsub, block, 0)


def _attention(proj, sinks_ordered, batch, seq):
    rows = min(ATTN_ROWS, seq)
    sub = rows // WINDOW
    nt = seq // rows
    T = batch * seq
    row = lambda b, n: b * nt + n
    prow = lambda b, n: (b * nt + n) * sub - jnp.minimum(n, 1)
    return pl.pallas_call(
        _attn_kernel,
        grid=(batch, nt),
        in_specs=[
            pl.BlockSpec(memory_space=pltpu.SMEM),
            pl.BlockSpec((rows, Q_DIM), lambda b, n: (row(b, n), OFF_Q // Q_DIM)),
            pl.BlockSpec((WINDOW, KV_DIM), lambda b, n: (prow(b, n), OFF_K // KV_DIM)),
            pl.BlockSpec((rows, KV_DIM), lambda b, n: (row(b, n), OFF_K // KV_DIM)),
            pl.BlockSpec((WINDOW, KV_DIM), lambda b, n: (prow(b, n), OFF_V // KV_DIM)),
            pl.BlockSpec((rows, KV_DIM), lambda b, n: (row(b, n), OFF_V // KV_DIM)),
        ],
        out_specs=pl.BlockSpec((rows, Q_DIM), lambda b, n: (row(b, n), 0)),
        out_shape=jax.ShapeDtypeStruct((T, Q_DIM), BF16),
        compiler_params=pltpu.CompilerParams(
            dimension_semantics=("arbitrary", "arbitrary"), vmem_limit_bytes=VMEM_LIMIT_BYTES),
        name="swa_attention",
    )(sinks_ordered, proj, proj, proj, proj, proj)


def _head_expand_matrix():
    k = jnp.arange(LANES)[:, None]
    c = jnp.arange(D_INNER)[None, :]
    return jnp.logical_and(k < 3 * N_SSD_HEADS, k % N_SSD_HEADS == c // SSD_HEAD_DIM).astype(BF16)


def _split3_lanes(v, lane):
    hi = v.astype(BF16).astype(F32)
    r1 = v - hi
    mid = r1.astype(BF16).astype(F32)
    parts = jnp.where(lane < N_SSD_HEADS, hi, jnp.where(lane < 2 * N_SSD_HEADS, mid, r1 - mid))
    return parts.astype(BF16)


def _ssd_kernel(xs_ref, b_ref, c_ref, dt_ref, dtT_ref,
                alog_ref, alogT_ref, dskip_ref, emat_ref,
                y_ref, state_ref):
    L = CHUNK

    @pl.when(pl.program_id(1) == 0)
    def _():
        state_ref[...] = jnp.zeros(state_ref.shape, F32)

    def chunk(i, carry):
        r0 = pl.multiple_of(i * L, L)
        rows = pl.ds(r0, L)
        dt = dt_ref[rows, :]
        dtT = dtT_ref[i]
        dA = dt * (-jnp.exp(alog_ref[...]))
        dAT = dtT * (-jnp.exp(alogT_ref[...]))

        ri = lax.broadcasted_iota(jnp.int32, (L, L), 0)
        ci = lax.broadcasted_iota(jnp.int32, (L, L), 1)
        causal = ci <= ri
        tri = jnp.where(causal, 1.0, 0.0).astype(BF16)
        triT = jnp.where(ri <= ci, 1.0, 0.0).astype(BF16)
        a_cs = sum(jnp.dot(tri, p, preferred_element_type=F32) for p in _split3(dA))
        a_csT = sum(jnp.dot(p, triT, preferred_element_type=F32) for p in _split3(dAT))

        a_last = a_cs[L - 1:L, :]
        ea = jnp.exp(a_cs)
        w_state = dt * jnp.exp(a_last - a_cs)
        a2 = a_cs * LOG2E
        a2T = (a_csT - jnp.log(dtT)) * LOG2E

        lane = lax.broadcasted_iota(jnp.int32, (L, LANES), 1)
        lane_lt_half = lane < SSD_HEAD_DIM
        ea_parts = _split3_lanes(ea, lane)
        ws_parts = _split3_lanes(w_state, lane)

        def group_inputs(g):
            gcol = slice(g * GROUP_W, (g + 1) * GROUP_W)
            ncol = slice(g * D_STATE, (g + 1) * D_STATE)
            bg_bf = b_ref[rows, ncol]
            cg_bf = c_ref[rows, ncol]
            cb = lax.dot_general(cg_bf, bg_bf, (((1,), (1,)), ((), ())), preferred_element_type=F32)
            ea_g = jnp.dot(ea_parts, emat_ref[:, gcol], preferred_element_type=F32)
            ws_g = jnp.dot(ws_parts, emat_ref[:, gcol], preferred_element_type=F32)
            y_off = jnp.dot(cg_bf, state_ref[g].astype(BF16), preferred_element_type=F32)
            return bg_bf, cb, ea_g, ws_g, y_off

        ready = group_inputs(0)
        for g in range(N_SSD_GROUPS):
            gcol = slice(g * GROUP_W, (g + 1) * GROUP_W)
            bg_bf, cb, ea_g, ws_g, y_off = ready
            if g + 1 < N_SSD_GROUPS:
                ready = group_inputs(g + 1)
            xs_bf = xs_ref[rows, gcol]
            xs = xs_bf.astype(F32)
            y_off = y_off * ea_g
            xw = (xs * ws_g).astype(BF16)
            bgT = bg_bf.astype(F32).T.astype(BF16)
            new_states = jnp.dot(bgT, xw, preferred_element_type=F32)
            state_ref[g] = state_ref[g] * ea_g[L - 1:L, :] + new_states
            y_pairs = []
            for jp in range(HEADS_PER_GROUP // 2):
                h0 = g * HEADS_PER_GROUP + 2 * jp
                xs_pair_bf = xs_bf[:, jp * LANES:(jp + 1) * LANES]
                yd = []
                for h in (h0, h0 + 1):
                    seg2 = a2[:, h:h + 1] - a2T[h:h + 1, :]
                    m = cb * jnp.exp2(jnp.where(causal, seg2, NEG))
                    yd.append(jnp.dot(m.astype(BF16), xs_pair_bf, preferred_element_type=F32))
                y_pairs.append(jnp.where(lane_lt_half, yd[0], yd[1])
                               + y_off[:, jp * LANES:(jp + 1) * LANES])
            y_ref[rows, gcol] = (jnp.concatenate(y_pairs, axis=-1) + xs * dskip_ref[:, gcol]).astype(BF16)
        return carry

    lax.fori_loop(0, xs_ref.shape[0] // L, chunk, 0)


def _ssd(proj, dt, dtT, a_log, d_skip, batch, seq):
    sub = min(SSD_SUB, seq // CHUNK)
    rows = sub * CHUNK
    nc = seq // rows
    T = batch * seq
    row = lambda b, n: b * nc + n
    alog_row = _tile3_heads(a_log.reshape(1, N_SSD_HEADS))
    alog_col = a_log.reshape(N_SSD_HEADS, 1)
    dskip_row = jnp.repeat(d_skip, SSD_HEAD_DIM).reshape(1, D_INNER)
    emat = _head_expand_matrix()
    return pl.pallas_call(
        _ssd_kernel,
        grid=(batch, nc),
        in_specs=[
            pl.BlockSpec((rows, D_INNER), lambda b, n: (row(b, n), OFF_XS // D_INNER)),
            pl.BlockSpec((rows, BC_DIM), lambda b, n: (row(b, n), OFF_B // BC_DIM)),
            pl.BlockSpec((rows, BC_DIM), lambda b, n: (row(b, n), OFF_C // BC_DIM)),
            pl.BlockSpec((rows, LANES), lambda b, n: (row(b, n), 0)),
            pl.BlockSpec((sub, N_SSD_HEADS, CHUNK), lambda b, n: (row(b, n), 0, 0)),
            _resident((1, LANES)), _resident((N_SSD_HEADS, 1)),
            _resident((1, D_INNER)),
            _resident(emat.shape),
        ],
        out_specs=pl.BlockSpec((rows, D_INNER), lambda b, n: (row(b, n), 0)),
        out_shape=jax.ShapeDtypeStruct((T, D_INNER), BF16),
        scratch_shapes=[pltpu.VMEM((N_SSD_GROUPS, D_STATE, GROUP_W), F32)],
        compiler_params=pltpu.CompilerParams(
            dimension_semantics=("arbitrary", "arbitrary"), vmem_limit_bytes=VMEM_LIMIT_BYTES),
        name="ssd_mixer",
    )(proj, proj, proj, dt, dtT, alog_row, alog_col, dskip_row, emat)


def _merge_kernel(x_ref, attn_ref, y_ref, z_ref, ga_ref, gs_ref, bg_ref, ynw_ref,
                  wa_ref, ws_ref, wo_ref, h_ref):
    attn = jnp.dot(attn_ref[...], wa_ref[...], preferred_element_type=F32)
    ssd = None
    for g in range(N_SSD_GROUPS):
        gcol = slice(g * GROUP_W, (g + 1) * GROUP_W)
        y = y_ref[:, gcol].astype(F32) * _silu_of_half(z_ref[:, gcol].astype(F32))
        ms = jnp.mean(y * y, axis=-1, keepdims=True)
        yn = (y * lax.rsqrt(ms + EPS) * ynw_ref[:, gcol]).astype(BF16)
        part = jnp.dot(yn, ws_ref[gcol, :], preferred_element_type=F32)
        ssd = part if ssd is None else ssd + part
    gate_a = _sigmoid(ga_ref[...].astype(F32) + bg_ref[:, :D_MODEL])
    gate_s = _sigmoid(gs_ref[...].astype(F32) + bg_ref[:, D_MODEL:])
    mixed = (gate_a * attn + gate_s * ssd).astype(BF16)
    h_ref[...] = x_ref[...] + jnp.dot(mixed, wo_ref[...], preferred_element_type=F32)


def _merge(x2d, attn, y, proj, b_gate, ssd_norm_w, w_attn_o, w_ssd_o, w_out):
    T = x2d.shape[0]
    tm = min(MERGE_TM, T)
    return pl.pallas_call(
        _merge_kernel,
        grid=(T // tm,),
        in_specs=[
            pl.BlockSpec((tm, D_MODEL), lambda i: (i, 0)),
            pl.BlockSpec((tm, Q_DIM), lambda i: (i, 0)),
            pl.BlockSpec((tm, D_INNER), lambda i: (i, 0)),
            pl.BlockSpec((tm, D_INNER), lambda i: (i, OFF_Z // D_INNER)),
            pl.BlockSpec((tm, D_MODEL), lambda i: (i, OFF_GA // D_MODEL)),
            pl.BlockSpec((tm, D_MODEL), lambda i: (i, OFF_GS // D_MODEL)),
            _resident((1, 2 * D_MODEL)),
            _resident((1, D_INNER)),
            _resident((Q_DIM, D_MODEL)), _resident((D_INNER, D_MODEL)), _resident((D_MODEL, D_MODEL)),
        ],
        out_specs=pl.BlockSpec((tm, D_MODEL), lambda i: (i, 0)),
        out_shape=jax.ShapeDtypeStruct((T, D_MODEL), F32),
        compiler_params=pltpu.CompilerParams(
            dimension_semantics=("arbitrary",), vmem_limit_bytes=VMEM_LIMIT_BYTES),
        name="gated_merge",
    )(x2d, attn, y, proj, proj, proj, b_gate, ssd_norm_w, w_attn_o, w_ssd_o, w_out)


def _ffn_kernel(h_ref, n2_ref, wup_ref, cw_ref, cb_ref, wdn_ref, fn_ref, o_ref,
                buf_ref, tail_ref, act_ref):
    h = h_ref[...]
    ms = jnp.mean(h * h, axis=-1, keepdims=True)
    hn = (h * lax.rsqrt(ms + EPS) * n2_ref[...]).astype(BF16)

    @pl.when(pl.program_id(1) == 0)
    def _():
        tail_ref[...] = jnp.zeros(tail_ref.shape, F32)

    for c in range(FFN_NCHUNK):
        vs = slice(c * FFN_CW, (c + 1) * FFN_CW)
        gs = slice(D_FF + c * FFN_CW, D_FF + (c + 1) * FFN_CW)
        u = jnp.concatenate([jnp.dot(hn, wup_ref[:, vs], preferred_element_type=F32),
                             jnp.dot(hn, wup_ref[:, gs], preferred_element_type=F32)], axis=-1)
        w = jnp.concatenate([cw_ref[:, vs], cw_ref[:, gs]], axis=-1)
        b = jnp.concatenate([cb_ref[:, vs], cb_ref[:, gs]], axis=-1)
        acc = _conv_rows(u, buf_ref, tail_ref.at[c], w, b, FFN_CONV)
        val = acc[:, :FFN_CW]
        half_gate = acc[:, FFN_CW:]
        act_ref[:, vs] = (_silu_of_half(half_gate) * val).astype(BF16)

    h2 = h + jnp.dot(act_ref[...], wdn_ref[...], preferred_element_type=F32)
    ms2 = jnp.mean(h2 * h2, axis=-1, keepdims=True)
    o_ref[...] = h2 * lax.rsqrt(ms2 + EPS) * fn_ref[...]


def _ffn(h2d, norm2_w, w_up, conv_w, conv_b, w_down, final_w, batch, seq):
    tm = min(FFN_TM, seq)
    nt = seq // tm
    T = batch * seq
    return pl.pallas_call(
        _ffn_kernel,
        grid=(batch, nt),
        in_specs=[
            pl.BlockSpec((tm, D_MODEL), lambda b, n: (b * nt + n, 0)),
            _resident((1, D_MODEL)),
            _resident((D_MODEL, 2 * D_FF)),
            _resident((FFN_CONV, 2 * D_FF)),
            _resident((1, 2 * D_FF)),
            _resident((D_FF, D_MODEL)),
            _resident((1, D_MODEL)),
        ],
        out_specs=pl.BlockSpec((tm, D_MODEL), lambda b, n: (b * nt + n, 0)),
        out_shape=jax.ShapeDtypeStruct((T, D_MODEL), F32),
        scratch_shapes=[
            pltpu.VMEM((SUBLANES + tm, 2 * FFN_CW), F32),
            pltpu.VMEM((FFN_NCHUNK, SUBLANES, 2 * FFN_CW), F32),
            pltpu.VMEM((tm, D_FF), BF16),
        ],
        compiler_params=pltpu.CompilerParams(
            dimension_semantics=("arbitrary", "arbitrary"), vmem_limit_bytes=VMEM_LIMIT_BYTES),
        name="conv_ffn",
    )(h2d, norm2_w, w_up, conv_w, conv_b, w_down, final_w)


def _scale_gate_half(t):
    return jnp.concatenate([t[..., :D_FF], 0.5 * t[..., D_FF:]], axis=-1)


def kernel(x, norm1_w, w_in, b_gate, attn_sinks, w_attn_o, ssd_conv_w, ssd_conv_b, dt_bias, a_log,
           d_skip, ssd_norm_w, w_ssd_o, w_out, norm2_w, w_up, ffn_conv_w, ffn_conv_b, w_down,
           final_norm_w):
    batch, seq, _ = x.shape
    T = batch * seq
    assert norm1_w.shape[0] == 1, "single-layer kernel"
    assert seq % ATTN_ROWS == 0 and seq % (SSD_SUB * CHUNK) == 0 and seq % IN_TM == 0

    w = w_in[0]
    o_q, o_k, o_v, o_z = 0, Q_DIM, Q_DIM + KV_DIM, Q_DIM + 2 * KV_DIM
    o_xs = o_z + D_INNER
    o_b = o_xs + D_INNER
    o_c = o_b + BC_DIM
    o_dt = o_c + BC_DIM
    o_ga = o_dt + N_SSD_HEADS
    o_gs = o_ga + D_MODEL
    cols = lambda a, n: w[:, a:a + n]
    w_q = jnp.concatenate([cols(o_q + hd * HEAD_DIM, HEAD_DIM) for hd in ATTN_HEAD_ORDER], axis=1)
    w_ao = w_attn_o[0].reshape(N_Q_HEADS, HEAD_DIM, D_MODEL)
    w_ao = jnp.concatenate([w_ao[hd] for hd in ATTN_HEAD_ORDER], axis=0).astype(BF16)
    sinks = jnp.stack([attn_sinks[0][hd] for hd in ATTN_HEAD_ORDER])
    w_main = jnp.concatenate([
        0.5 * cols(o_z, D_INNER), cols(o_xs, D_INNER), w_q, cols(o_ga, D_MODEL),
        cols(o_gs, D_MODEL), cols(o_b, BC_DIM), cols(o_c, BC_DIM), cols(o_k, KV_DIM),
        cols(o_v, KV_DIM)], axis=1).astype(BF16)
    w_dt = cols(o_dt, N_SSD_HEADS)
    w_dt_pad = _tile3_heads(w_dt).astype(BF16)
    w_dtT = w_dt.T.astype(BF16)
    dtb_row = _tile3_heads(dt_bias[0].reshape(1, N_SSD_HEADS))
    dtb_col = dt_bias[0].reshape(N_SSD_HEADS, 1)

    x2d = x.reshape(T, D_MODEL)
    proj, dt, dtT = _inproj(x2d, norm1_w[0].reshape(1, D_MODEL), w_main, 0.5 * ssd_conv_w[0],
                            0.5 * ssd_conv_b[0].reshape(1, XBC_DIM), w_dt_pad, w_dtT, dtb_row, dtb_col, seq)
    attn = _attention(proj, sinks, batch, seq)
    y = _ssd(proj, dt, dtT, a_log[0], d_skip[0], batch, seq)
    h = _merge(x2d, attn, y, proj, b_gate[0].reshape(1, 2 * D_MODEL), ssd_norm_w[0].reshape(1, D_INNER), w_ao,
               w_ssd_o[0].astype(BF16), w_out[0].astype(BF16))
    out = _ffn(h, norm2_w[0].reshape(1, D_MODEL), w_up[0].astype(BF16),
               _scale_gate_half(ffn_conv_w[0]), _scale_gate_half(ffn_conv_b[0].reshape(1, 2 * D_FF)),
               w_down[0].astype(BF16), final_norm_w.reshape(1, D_MODEL), batch, seq)
    return out.reshape(batch, seq, D_MODEL)
```

```python
import functools

import jax
import jax.numpy as jnp
from jax import lax
from jax.experimental import pallas as pl
from jax.experimental.pallas import tpu as pltpu

F32 = jnp.float32
BF16 = jnp.bfloat16

D_MODEL = 1024
N_Q_HEADS = 16
N_KV_HEADS = 4
Q_PER_KV = N_Q_HEADS // N_KV_HEADS
HEAD_DIM = 64
WINDOW = 128
D_INNER = 2048
SSD_HEAD_DIM = 64
N_SSD_HEADS = 32
N_SSD_GROUPS = 4
HEADS_PER_GROUP = N_SSD_HEADS // N_SSD_GROUPS
D_STATE = 128
SSD_CONV = 4
CHUNK = 128
D_FF = 2816
FFN_CONV = 3
EPS = 1e-5
NEG = -1e30
LOG2E = 1.4426950408889634
Q_DIM = N_Q_HEADS * HEAD_DIM
KV_DIM = N_KV_HEADS * HEAD_DIM
BC_DIM = N_SSD_GROUPS * D_STATE
XBC_DIM = D_INNER + 2 * BC_DIM
GROUP_W = D_INNER // N_SSD_GROUPS

LANES = 128
SUBLANES = 8
VMEM_LIMIT_BYTES = 56 * 1024 * 1024

OFF_Z = 0
OFF_XS = OFF_Z + D_INNER
OFF_Q = OFF_XS + D_INNER
OFF_GA = OFF_Q + Q_DIM
OFF_GS = OFF_GA + D_MODEL
OFF_B = OFF_GS + D_MODEL
OFF_C = OFF_B + BC_DIM
OFF_K = OFF_C + BC_DIM
OFF_V = OFF_K + KV_DIM
PROJ_W = OFF_V + KV_DIM

IN_TM = 512
IN_TN = 512
CONV_CHUNKS = {OFF_XS // IN_TN + k: k for k in range(D_INNER // IN_TN)}
CONV_CHUNKS[OFF_B // IN_TN] = D_INNER // IN_TN
CONV_CHUNKS[OFF_C // IN_TN] = D_INNER // IN_TN + 1
_PLAIN_CHUNKS = [c for c in range(PROJ_W // IN_TN) if c not in CONV_CHUNKS]
IN_CHUNK_ORDER = tuple(c for pair in zip(sorted(CONV_CHUNKS), _PLAIN_CHUNKS) for c in pair) \
    + tuple(_PLAIN_CHUNKS[len(CONV_CHUNKS):])
ATTN_ROWS = 1024
SSD_SUB = 8
MERGE_TM = 512
FFN_TM = 1024
FFN_CW = 256
FFN_NCHUNK = D_FF // FFN_CW


def _silu_of_half(h):
    return h + h * jnp.tanh(h)


def _sigmoid(x):
    return 1.0 / (1.0 + jnp.exp(-x))


def _softplus(x):
    return jnp.maximum(x, 0.0) + jnp.log(1.0 + jnp.exp(-jnp.abs(x)))


def _tile3_heads(t):
    pad = jnp.zeros(t.shape[:-1] + (LANES - 3 * N_SSD_HEADS,), t.dtype)
    return jnp.concatenate([t, t, t, pad], axis=-1)


def _split3(x):
    hi = x.astype(BF16)
    r1 = x - hi.astype(F32)
    mid = r1.astype(BF16)
    lo = (r1 - mid.astype(F32)).astype(BF16)
    return hi, mid, lo


def _resident(shape):
    return pl.BlockSpec(shape, lambda *_: (0,) * len(shape), pipeline_mode=pl.Buffered(1))


def _conv_stage(u, buf_ref, tail_ref):
    tm = u.shape[0]
    buf_ref[0:SUBLANES, :] = tail_ref[...]
    buf_ref[SUBLANES:SUBLANES + tm, :] = u
    tail_ref[...] = u[tm - SUBLANES:tm, :]


def _conv_finish(buf_ref, w, b, taps, u=None):
    tm = buf_ref.shape[0] - SUBLANES
    if u is None:
        u = buf_ref[SUBLANES:SUBLANES + tm, :]
    acc = u * w[taps - 1:taps, :] + b
    for k in range(taps - 1):
        off = SUBLANES - (taps - 1) + k
        acc = acc + buf_ref[off:off + tm, :] * w[k:k + 1, :]
    return acc


def _inproj_kernel(x_ref, nw_ref, w_ref, cw_ref, cb_ref, wdt_ref, wdtT_ref, dtb_ref, dtbT_ref,
                   proj_ref, dt_ref, dtT_ref, buf_ref, tail_ref, *, tiles_per_seq):
    @pl.when(pl.program_id(0) % tiles_per_seq == 0)
    def _():
        tail_ref[...] = jnp.zeros(tail_ref.shape, F32)

    x = x_ref[...]
    ms = jnp.mean(x * x, axis=-1, keepdims=True)
    xn = (x * lax.rsqrt(ms + EPS) * nw_ref[...]).astype(BF16)

    def finish_conv(c):
        k = CONV_CHUNKS[c]
        ks = slice(k * IN_TN, (k + 1) * IN_TN)
        acc = _conv_finish(buf_ref.at[k % 2], cw_ref[:, ks], cb_ref[:, ks], SSD_CONV)
        proj_ref[:, c * IN_TN:(c + 1) * IN_TN] = _silu_of_half(acc).astype(BF16)

    pending = None
    for c in IN_CHUNK_ORDER:
        cs = slice(c * IN_TN, (c + 1) * IN_TN)
        u = jnp.dot(xn, w_ref[:, cs], preferred_element_type=F32)
        if c in CONV_CHUNKS:
            k = CONV_CHUNKS[c]
            _conv_stage(u, buf_ref.at[k % 2], tail_ref.at[k])
        else:
            proj_ref[:, cs] = u.astype(BF16)
        if pending is not None:
            finish_conv(pending)
        pending = c if c in CONV_CHUNKS else None
    assert pending is None
    dt_raw = jnp.dot(xn, wdt_ref[...], preferred_element_type=F32)
    dt_ref[...] = _softplus(dt_raw + dtb_ref[...])
    dtT_raw = lax.dot_general(wdtT_ref[...], xn, (((1,), (1,)), ((), ())),
                              preferred_element_type=F32)
    dtT = _softplus(dtT_raw + dtbT_ref[...])
    for c in range(dtT_ref.shape[0]):
        dtT_ref[c] = dtT[:, c * CHUNK:(c + 1) * CHUNK]


def _inproj(x2d, norm_w, w_main, conv_w_half, conv_b_half, w_dt, w_dtT, dt_bias_row, dt_bias_col, seq):
    T = x2d.shape[0]
    tm = min(IN_TM, seq)
    return pl.pallas_call(
        functools.partial(_inproj_kernel, tiles_per_seq=seq // tm),
        grid=(T // tm,),
        in_specs=[
            pl.BlockSpec((tm, D_MODEL), lambda i: (i, 0)),
            _resident((1, D_MODEL)),
            _resident((D_MODEL, PROJ_W)),
            _resident((SSD_CONV, XBC_DIM)),
            _resident((1, XBC_DIM)),
            _resident((D_MODEL, LANES)),
            _resident((N_SSD_HEADS, D_MODEL)),
            _resident((1, LANES)),
            _resident((N_SSD_HEADS, 1)),
        ],
        out_specs=[
            pl.BlockSpec((tm, PROJ_W), lambda i: (i, 0)),
            pl.BlockSpec((tm, LANES), lambda i: (i, 0)),
            pl.BlockSpec((tm // CHUNK, N_SSD_HEADS, CHUNK), lambda i: (i, 0, 0)),
        ],
        out_shape=[
            jax.ShapeDtypeStruct((T, PROJ_W), BF16),
            jax.ShapeDtypeStruct((T, LANES), F32),
            jax.ShapeDtypeStruct((T // CHUNK, N_SSD_HEADS, CHUNK), F32),
        ],
        scratch_shapes=[
            pltpu.VMEM((2, SUBLANES + tm, IN_TN), F32),
            pltpu.VMEM((XBC_DIM // IN_TN, SUBLANES, IN_TN), F32),
        ],
        compiler_params=pltpu.CompilerParams(
            dimension_semantics=("arbitrary",), vmem_limit_bytes=VMEM_LIMIT_BYTES),
        name="inproj",
    )(x2d, norm_w, w_main, conv_w_half, conv_b_half, w_dt, w_dtT, dt_bias_row, dt_bias_col)


ATTN_HEAD_ORDER = tuple(
    (2 * (j // Q_PER_KV) + half) * Q_PER_KV + j % Q_PER_KV
    for j in range(N_Q_HEADS // 2) for half in range(2))


def _attn_kernel(sinks_ref, q_ref, kp_ref, kc_ref, vp_ref, vc_ref, o_ref):
    W = WINDOW
    nsub = q_ref.shape[0] // W
    qi = lax.broadcasted_iota(jnp.int32, (W, 2 * W), 0)
    si = lax.broadcasted_iota(jnp.int32, (W, 2 * W), 1)
    in_prev = jnp.logical_and(si < W, si > qi)
    in_cur = jnp.logical_and(si >= W, si - W <= qi)
    lane = lax.broadcasted_iota(jnp.int32, (W, LANES), 1)
    left = lane < HEAD_DIM

    def block(i, carry):
        r0 = pl.multiple_of(i * W, W)
        rows = pl.ds(r0, W)
        before = pl.ds(pl.multiple_of(jnp.maximum(r0 - W, 0), W), W)
        has_prev = jnp.logical_or(pl.program_id(1) > 0, i > 0)
        valid = jnp.logical_or(jnp.logical_and(in_prev, has_prev), in_cur)
        scores, vpairs = [], []
        for kv in range(N_KV_HEADS // 2):
            kvc = slice(kv * LANES, (kv + 1) * LANES)
            k_prev = jnp.where(i > 0, kc_ref[before, kvc], kp_ref[:, kvc])
            v_prev = jnp.where(i > 0, vc_ref[before, kvc], vp_ref[:, kvc])
            kpair = jnp.concatenate([k_prev, kc_ref[rows, kvc]], axis=0)
            vpairs.append(jnp.concatenate([v_prev, vc_ref[rows, kvc]], axis=0))
            for g in range(Q_PER_KV):
                j = kv * Q_PER_KV + g
                col = slice(j * LANES, (j + 1) * LANES)
                qc = q_ref[rows, col].astype(F32) * (HEAD_DIM ** -0.5)
                q2 = jnp.concatenate([jnp.where(left, qc, 0.0), jnp.where(left, 0.0, qc)],
                                     axis=0).astype(BF16)
                scores.append(lax.dot_general(q2, kpair, (((1,), (1,)), ((), ())),
                                              preferred_element_type=F32))
        for kv in range(N_KV_HEADS // 2):
            vpair = vpairs[kv]
            for g in range(Q_PER_KV):
                j = kv * Q_PER_KV + g
                col = slice(j * LANES, (j + 1) * LANES)
                s2 = scores[j]
                ps, rs = [], []
                for half in range(2):
                    s = jnp.where(valid, s2[half * W:(half + 1) * W, :], NEG)
                    sink = sinks_ref[2 * j + half]
                    m = jnp.maximum(jnp.max(s, axis=-1, keepdims=True), sink)
                    p = jnp.exp(s - m)
                    denom = jnp.sum(p, axis=-1, keepdims=True) + jnp.exp(sink - m)
                    ps.append(p.astype(BF16))
                    rs.append(1.0 / denom)
                o2 = jnp.dot(jnp.concatenate(ps, axis=0), vpair, preferred_element_type=F32)
                o = jnp.where(left, o2[:W, :] * rs[0], o2[W:, :] * rs[1])
                o_ref[rows, col] = o.astype(BF16)
        return carry

    lax.fori_loop(0, nsub, block, 0)


def _attention(proj, sinks_ordered, batch, seq):
    rows = min(ATTN_ROWS, seq)
    sub = rows // WINDOW
    nt = seq // rows
    T = batch * seq
    row = lambda b, n: b * nt + n
    prow = lambda b, n: (b * nt + n) * sub - jnp.minimum(n, 1)
    return pl.pallas_call(
        _attn_kernel,
        grid=(batch, nt),
        in_specs=[
            pl.BlockSpec(memory_space=pltpu.SMEM),
            pl.BlockSpec((rows, Q_DIM), lambda b, n: (row(b, n), OFF_Q // Q_DIM)),
            pl.BlockSpec((WINDOW, KV_DIM), lambda b, n: (prow(b, n), OFF_K // KV_DIM)),
            pl.BlockSpec((rows, KV_DIM), lambda b, n: (row(b, n), OFF_K // KV_DIM)),
            pl.BlockSpec((WINDOW, KV_DIM), lambda b, n: (prow(b, n), OFF_V // KV_DIM)),
            pl.BlockSpec((rows, KV_DIM), lambda b, n: (row(b, n), OFF_V // KV_DIM)),
        ],
        out_specs=pl.BlockSpec((rows, Q_DIM), lambda b, n: (row(b, n), 0)),
        out_shape=jax.ShapeDtypeStruct((T, Q_DIM), BF16),
        compiler_params=pltpu.CompilerParams(
            dimension_semantics=("arbitrary", "arbitrary"), vmem_limit_bytes=VMEM_LIMIT_BYTES),
        name="swa_attention",
    )(sinks_ordered, proj, proj, proj, proj, proj)


def _head_expand_matrix():
    k = jnp.arange(LANES)[:, None]
    c = jnp.arange(D_INNER)[None, :]
    return jnp.logical_and(k < 3 * N_SSD_HEADS, k % N_SSD_HEADS == c // SSD_HEAD_DIM).astype(BF16)


def _split3_lanes(v, lane):
    hi = v.astype(BF16).astype(F32)
    r1 = v - hi
    mid = r1.astype(BF16).astype(F32)
    parts = jnp.where(lane < N_SSD_HEADS, hi, jnp.where(lane < 2 * N_SSD_HEADS, mid, r1 - mid))
    return parts.astype(BF16)


def _ssd_kernel(xs_ref, b_ref, c_ref, dt_ref, dtT_ref,
                alog_ref, alogT_ref, dskip_ref, emat_ref,
                y_ref, state_ref):
    L = CHUNK

    @pl.when(pl.program_id(1) == 0)
    def _():
        state_ref[...] = jnp.zeros(state_ref.shape, F32)

    def chunk(i, carry):
        r0 = pl.multiple_of(i * L, L)
        rows = pl.ds(r0, L)
        dt = dt_ref[rows, :]
        dtT = dtT_ref[i]
        dA = dt * (-jnp.exp(alog_ref[...]))
        dAT = dtT * (-jnp.exp(alogT_ref[...]))

        ri = lax.broadcasted_iota(jnp.int32, (L, L), 0)
        ci = lax.broadcasted_iota(jnp.int32, (L, L), 1)
        causal = ci <= ri
        tri = jnp.where(causal, 1.0, 0.0).astype(BF16)
        triT = jnp.where(ri <= ci, 1.0, 0.0).astype(BF16)
        a_cs = sum(jnp.dot(tri, p, preferred_element_type=F32) for p in _split3(dA))
        a_csT = sum(jnp.dot(p, triT, preferred_element_type=F32) for p in _split3(dAT))

        a_last = a_cs[L - 1:L, :]
        ea = jnp.exp(a_cs)
        w_state = dt * jnp.exp(a_last - a_cs)
        a2 = a_cs * LOG2E
        a2T = (a_csT - jnp.log(dtT)) * LOG2E

        lane = lax.broadcasted_iota(jnp.int32, (L, LANES), 1)
        lane_lt_half = lane < SSD_HEAD_DIM
        ea_parts = _split3_lanes(ea, lane)
        ws_parts = _split3_lanes(w_state, lane)

        def group_inputs(g):
            gcol = slice(g * GROUP_W, (g + 1) * GROUP_W)
            ncol = slice(g * D_STATE, (g + 1) * D_STATE)
            bg_bf = b_ref[rows, ncol]
            cg_bf = c_ref[rows, ncol]
            cb = lax.dot_general(cg_bf, bg_bf, (((1,), (1,)), ((), ())), preferred_element_type=F32)
            ea_g = jnp.dot(ea_parts, emat_ref[:, gcol], preferred_element_type=F32)
            ws_g = jnp.dot(ws_parts, emat_ref[:, gcol], preferred_element_type=F32)
            y_off = jnp.dot(cg_bf, state_ref[g].astype(BF16), preferred_element_type=F32)
            return bg_bf, cb, ea_g, ws_g, y_off

        ready = group_inputs(0)
        for g in range(N_SSD_GROUPS):
            gcol = slice(g * GROUP_W, (g + 1) * GROUP_W)
            bg_bf, cb, ea_g, ws_g, y_off = ready
            if g + 1 < N_SSD_GROUPS:
                ready = group_inputs(g + 1)
            xs_bf = xs_ref[rows, gcol]
            xs = xs_bf.astype(F32)
            y_off = y_off * ea_g
            xw = (xs * ws_g).astype(BF16)
            bgT = bg_bf.astype(F32).T.astype(BF16)
            new_states = jnp.dot(bgT, xw, preferred_element_type=F32)
            state_ref[g] = state_ref[g] * ea_g[L - 1:L, :] + new_states
            y_pairs = []
            for jp in range(HEADS_PER_GROUP // 2):
                h0 = g * HEADS_PER_GROUP + 2 * jp
                xs_pair_bf = xs_bf[:, jp * LANES:(jp + 1) * LANES]
                yd = []
                for h in (h0, h0 + 1):
                    seg2 = a2[:, h:h + 1] - a2T[h:h + 1, :]
                    m = cb * jnp.exp2(jnp.where(causal, seg2, NEG))
                    yd.append(jnp.dot(m.astype(BF16), xs_pair_bf, preferred_element_type=F32))
                y_pairs.append(jnp.where(lane_lt_half, yd[0], yd[1])
                               + y_off[:, jp * LANES:(jp + 1) * LANES])
            y_ref[rows, gcol] = (jnp.concatenate(y_pairs, axis=-1) + xs * dskip_ref[:, gcol]).astype(BF16)
        return carry

    lax.fori_loop(0, xs_ref.shape[0] // L, chunk, 0)


def _ssd(proj, dt, dtT, a_log, d_skip, batch, seq):
    sub = min(SSD_SUB, seq // CHUNK)
    rows = sub * CHUNK
    nc = seq // rows
    T = batch * seq
    row = lambda b, n: b * nc + n
    alog_row = _tile3_heads(a_log.reshape(1, N_SSD_HEADS))
    alog_col = a_log.reshape(N_SSD_HEADS, 1)
    dskip_row = jnp.repeat(d_skip, SSD_HEAD_DIM).reshape(1, D_INNER)
    emat = _head_expand_matrix()
    return pl.pallas_call(
        _ssd_kernel,
        grid=(batch, nc),
        in_specs=[
            pl.BlockSpec((rows, D_INNER), lambda b, n: (row(b, n), OFF_XS // D_INNER)),
            pl.BlockSpec((rows, BC_DIM), lambda b, n: (row(b, n), OFF_B // BC_DIM)),
            pl.BlockSpec((rows, BC_DIM), lambda b, n: (row(b, n), OFF_C // BC_DIM)),
            pl.BlockSpec((rows, LANES), lambda b, n: (row(b, n), 0)),
            pl.BlockSpec((sub, N_SSD_HEADS, CHUNK), lambda b, n: (row(b, n), 0, 0)),
            _resident((1, LANES)), _resident((N_SSD_HEADS, 1)),
            _resident((1, D_INNER)),
            _resident(emat.shape),
        ],
        out_specs=pl.BlockSpec((rows, D_INNER), lambda b, n: (row(b, n), 0)),
        out_shape=jax.ShapeDtypeStruct((T, D_INNER), BF16),
        scratch_shapes=[pltpu.VMEM((N_SSD_GROUPS, D_STATE, GROUP_W), F32)],
        compiler_params=pltpu.CompilerParams(
            dimension_semantics=("arbitrary", "arbitrary"), vmem_limit_bytes=VMEM_LIMIT_BYTES),
        name="ssd_mixer",
    )(proj, proj, proj, dt, dtT, alog_row, alog_col, dskip_row, emat)


def _merge_kernel(x_ref, attn_ref, y_ref, z_ref, ga_ref, gs_ref, bg_ref, ynw_ref,
                  wa_ref, ws_ref, wo_ref, h_ref):
    attn = jnp.dot(attn_ref[...], wa_ref[...], preferred_element_type=F32)
    ssd = None
    for g in range(N_SSD_GROUPS):
        gcol = slice(g * GROUP_W, (g + 1) * GROUP_W)
        y = y_ref[:, gcol].astype(F32) * _silu_of_half(z_ref[:, gcol].astype(F32))
        ms = jnp.mean(y * y, axis=-1, keepdims=True)
        yn = (y * lax.rsqrt(ms + EPS) * ynw_ref[:, gcol]).astype(BF16)
        part = jnp.dot(yn, ws_ref[gcol, :], preferred_element_type=F32)
        ssd = part if ssd is None else ssd + part
    gate_a = _sigmoid(ga_ref[...].astype(F32) + bg_ref[:, :D_MODEL])
    gate_s = _sigmoid(gs_ref[...].astype(F32) + bg_ref[:, D_MODEL:])
    mixed = (gate_a * attn + gate_s * ssd).astype(BF16)
    h_ref[...] = x_ref[...] + jnp.dot(mixed, wo_ref[...], preferred_element_type=F32)


def _merge(x2d, attn, y, proj, b_gate, ssd_norm_w, w_attn_o, w_ssd_o, w_out):
    T = x2d.shape[0]
    tm = min(MERGE_TM, T)
    return pl.pallas_call(
        _merge_kernel,
        grid=(T // tm,),
        in_specs=[
            pl.BlockSpec((tm, D_MODEL), lambda i: (i, 0)),
            pl.BlockSpec((tm, Q_DIM), lambda i: (i, 0)),
            pl.BlockSpec((tm, D_INNER), lambda i: (i, 0)),
            pl.BlockSpec((tm, D_INNER), lambda i: (i, OFF_Z // D_INNER)),
            pl.BlockSpec((tm, D_MODEL), lambda i: (i, OFF_GA // D_MODEL)),
            pl.BlockSpec((tm, D_MODEL), lambda i: (i, OFF_GS // D_MODEL)),
            _resident((1, 2 * D_MODEL)),
            _resident((1, D_INNER)),
            _resident((Q_DIM, D_MODEL)), _resident((D_INNER, D_MODEL)), _resident((D_MODEL, D_MODEL)),
        ],
        out_specs=pl.BlockSpec((tm, D_MODEL), lambda i: (i, 0)),
        out_shape=jax.ShapeDtypeStruct((T, D_MODEL), F32),
        compiler_params=pltpu.CompilerParams(
            dimension_semantics=("arbitrary",), vmem_limit_bytes=VMEM_LIMIT_BYTES),
        name="gated_merge",
    )(x2d, attn, y, proj, proj, proj, b_gate, ssd_norm_w, w_attn_o, w_ssd_o, w_out)


def _ffn_kernel(h_ref, n2_ref, wup_ref, cw_ref, cb_ref, wdn_ref, fn_ref, o_ref,
                buf_ref, tail_ref, act_ref):
    h = h_ref[...]
    ms = jnp.mean(h * h, axis=-1, keepdims=True)
    hn = (h * lax.rsqrt(ms + EPS) * n2_ref[...]).astype(BF16)

    @pl.when(pl.program_id(1) == 0)
    def _():
        tail_ref[...] = jnp.zeros(tail_ref.shape, F32)

    def cols(c):
        return slice(c * FFN_CW, (c + 1) * FFN_CW), slice(D_FF + c * FFN_CW, D_FF + (c + 1) * FFN_CW)

    for c in range(FFN_NCHUNK):
        vs, gs = cols(c)
        u = jnp.concatenate([jnp.dot(hn, wup_ref[:, vs], preferred_element_type=F32),
                             jnp.dot(hn, wup_ref[:, gs], preferred_element_type=F32)], axis=-1)
        _conv_stage(u, buf_ref, tail_ref.at[c])
        w = jnp.concatenate([cw_ref[:, vs], cw_ref[:, gs]], axis=-1)
        b = jnp.concatenate([cb_ref[:, vs], cb_ref[:, gs]], axis=-1)
        acc = _conv_finish(buf_ref, w, b, FFN_CONV, u)
        act_ref[:, vs] = (_silu_of_half(acc[:, FFN_CW:]) * acc[:, :FFN_CW]).astype(BF16)

    h2 = h + jnp.dot(act_ref[...], wdn_ref[...], preferred_element_type=F32)
    ms2 = jnp.mean(h2 * h2, axis=-1, keepdims=True)
    o_ref[...] = h2 * lax.rsqrt(ms2 + EPS) * fn_ref[...]


def _ffn(h2d, norm2_w, w_up, conv_w, conv_b, w_down, final_w, batch, seq):
    tm = min(FFN_TM, seq)
    nt = seq // tm
    T = batch * seq
    return pl.pallas_call(
        _ffn_kernel,
        grid=(batch, nt),
        in_specs=[
            pl.BlockSpec((tm, D_MODEL), lambda b, n: (b * nt + n, 0)),
            _resident((1, D_MODEL)),
            _resident((D_MODEL, 2 * D_FF)),
            _resident((FFN_CONV, 2 * D_FF)),
            _resident((1, 2 * D_FF)),
            _resident((D_FF, D_MODEL)),
            _resident((1, D_MODEL)),
        ],
        out_specs=pl.BlockSpec((tm, D_MODEL), lambda b, n: (b * nt + n, 0)),
        out_shape=jax.ShapeDtypeStruct((T, D_MODEL), F32),
        scratch_shapes=[
            pltpu.VMEM((SUBLANES + tm, 2 * FFN_CW), F32),
            pltpu.VMEM((FFN_NCHUNK, SUBLANES, 2 * FFN_CW), F32),
            pltpu.VMEM((tm, D_FF), BF16),
        ],
        compiler_params=pltpu.CompilerParams(
            dimension_semantics=("arbitrary", "arbitrary"), vmem_limit_bytes=VMEM_LIMIT_BYTES),
        name="conv_ffn",
    )(h2d, norm2_w, w_up, conv_w, conv_b, w_down, final_w)


def _scale_gate_half(t):
    return jnp.concatenate([t[..., :D_FF], 0.5 * t[..., D_FF:]], axis=-1)


IN_O_Q = 0
IN_O_K = IN_O_Q + Q_DIM
IN_O_V = IN_O_K + KV_DIM
IN_O_Z = IN_O_V + KV_DIM
IN_O_XBC = IN_O_Z + D_INNER
IN_O_DT = IN_O_XBC + XBC_DIM
IN_O_GA = IN_O_DT + N_SSD_HEADS
IN_O_GS = IN_O_GA + D_MODEL
IN_DIM = IN_O_GS + D_MODEL
RELAYOUT_ROWS = 128


def _relayout_kernel(w_ref, o_ref):
    def put(dst, v):
        o_ref[:, dst:dst + v.shape[1]] = v.astype(BF16)

    put(OFF_Z, 0.5 * w_ref[:, IN_O_Z:IN_O_Z + D_INNER])
    put(OFF_XS, w_ref[:, IN_O_XBC:IN_O_XBC + D_INNER])
    q = w_ref[:, IN_O_Q:IN_O_Q + Q_DIM]
    put(OFF_Q, jnp.concatenate([q[:, hd * HEAD_DIM:(hd + 1) * HEAD_DIM] for hd in ATTN_HEAD_ORDER], axis=1))
    tail = w_ref[:, IN_O_DT:IN_DIM]
    put(OFF_GA, tail[:, N_SSD_HEADS:N_SSD_HEADS + D_MODEL])
    put(OFF_GS, tail[:, N_SSD_HEADS + D_MODEL:N_SSD_HEADS + 2 * D_MODEL])
    put(OFF_B, w_ref[:, IN_O_XBC + D_INNER:IN_O_XBC + D_INNER + BC_DIM])
    put(OFF_C, w_ref[:, IN_O_XBC + D_INNER + BC_DIM:IN_O_XBC + XBC_DIM])
    put(OFF_K, w_ref[:, IN_O_K:IN_O_K + KV_DIM])
    put(OFF_V, w_ref[:, IN_O_V:IN_O_V + KV_DIM])


def _relayout_w_in(w):
    return pl.pallas_call(
        _relayout_kernel,
        grid=(D_MODEL // RELAYOUT_ROWS,),
        in_specs=[pl.BlockSpec((RELAYOUT_ROWS, IN_DIM), lambda i: (i, 0))],
        out_specs=pl.BlockSpec((RELAYOUT_ROWS, PROJ_W), lambda i: (i, 0)),
        out_shape=jax.ShapeDtypeStruct((D_MODEL, PROJ_W), BF16),
        compiler_params=pltpu.CompilerParams(
            dimension_semantics=("arbitrary",), vmem_limit_bytes=VMEM_LIMIT_BYTES),
        name="relayout_w_in",
    )(w)


def kernel(x, norm1_w, w_in, b_gate, attn_sinks, w_attn_o, ssd_conv_w, ssd_conv_b, dt_bias, a_log,
           d_skip, ssd_norm_w, w_ssd_o, w_out, norm2_w, w_up, ffn_conv_w, ffn_conv_b, w_down,
           final_norm_w):
    batch, seq, _ = x.shape
    T = batch * seq
    assert norm1_w.shape[0] == 1, "single-layer kernel"
    assert seq % ATTN_ROWS == 0 and seq % (SSD_SUB * CHUNK) == 0 and seq % IN_TM == 0

    w = w_in[0]
    head_order = jnp.asarray(ATTN_HEAD_ORDER, jnp.int32)
    w_ao = jnp.take(w_attn_o[0].reshape(N_Q_HEADS, HEAD_DIM, D_MODEL), head_order, axis=0)
    w_ao = w_ao.reshape(Q_DIM, D_MODEL).astype(BF16)
    sinks = jnp.take(attn_sinks[0], head_order)
    w_main = _relayout_w_in(w)
    w_dt = w[:, IN_O_DT:IN_O_DT + N_SSD_HEADS]
    w_dt_pad = _tile3_heads(w_dt).astype(BF16)
    w_dtT = w_dt.T.astype(BF16)
    dtb_row = _tile3_heads(dt_bias[0].reshape(1, N_SSD_HEADS))
    dtb_col = dt_bias[0].reshape(N_SSD_HEADS, 1)

    x2d = x.reshape(T, D_MODEL)
    proj, dt, dtT = _inproj(x2d, norm1_w[0].reshape(1, D_MODEL), w_main, 0.5 * ssd_conv_w[0],
                            0.5 * ssd_conv_b[0].reshape(1, XBC_DIM), w_dt_pad, w_dtT, dtb_row, dtb_col, seq)
    attn = _attention(proj, sinks, batch, seq)
    y = _ssd(proj, dt, dtT, a_log[0], d_skip[0], batch, seq)
    h = _merge(x2d, attn, y, proj, b_gate[0].reshape(1, 2 * D_MODEL), ssd_norm_w[0].reshape(1, D_INNER), w_ao,
               w_ssd_o[0].astype(BF16), w_out[0].astype(BF16))
    out = _ffn(h, norm2_w[0].reshape(1, D_MODEL), w_up[0].astype(BF16),
               _scale_gate_half(ffn_conv_w[0]), _scale_gate_half(ffn_conv_b[0].reshape(1, 2 * D_FF)),
               w_down[0].astype(BF16), final_norm_w.reshape(1, D_MODEL), batch, seq)
    return out.reshape(batch, seq, D_MODEL)
```

```python
import functools

import jax
import jax.numpy as jnp
from jax import lax
from jax.experimental import pallas as pl
from jax.experimental.pallas import tpu as pltpu

F32 = jnp.float32
BF16 = jnp.bfloat16

D_MODEL = 1024
N_Q_HEADS = 16
N_KV_HEADS = 4
Q_PER_KV = N_Q_HEADS // N_KV_HEADS
HEAD_DIM = 64
WINDOW = 128
D_INNER = 2048
SSD_HEAD_DIM = 64
N_SSD_HEADS = 32
N_SSD_GROUPS = 4
HEADS_PER_GROUP = N_SSD_HEADS // N_SSD_GROUPS
D_STATE = 128
SSD_CONV = 4
CHUNK = 128
D_FF = 2816
FFN_CONV = 3
EPS = 1e-5
NEG = -1e30
LOG2E = 1.4426950408889634
Q_DIM = N_Q_HEADS * HEAD_DIM
KV_DIM = N_KV_HEADS * HEAD_DIM
BC_DIM = N_SSD_GROUPS * D_STATE
XBC_DIM = D_INNER + 2 * BC_DIM
GROUP_W = D_INNER // N_SSD_GROUPS

LANES = 128
SUBLANES = 8
VMEM_LIMIT_BYTES = 56 * 1024 * 1024

OFF_Z = 0
OFF_XS = OFF_Z + D_INNER
OFF_Q = OFF_XS + D_INNER
OFF_GA = OFF_Q + Q_DIM
OFF_GS = OFF_GA + D_MODEL
OFF_B = OFF_GS + D_MODEL
OFF_C = OFF_B + BC_DIM
OFF_K = OFF_C + BC_DIM
OFF_V = OFF_K + KV_DIM
PROJ_W = OFF_V + KV_DIM

IN_TM = 512
IN_TN = 512
CONV_CHUNKS = {OFF_XS // IN_TN + k: k for k in range(D_INNER // IN_TN)}
CONV_CHUNKS[OFF_B // IN_TN] = D_INNER // IN_TN
CONV_CHUNKS[OFF_C // IN_TN] = D_INNER // IN_TN + 1
_PLAIN_CHUNKS = [c for c in range(PROJ_W // IN_TN) if c not in CONV_CHUNKS]
IN_CHUNK_ORDER = tuple(c for pair in zip(sorted(CONV_CHUNKS), _PLAIN_CHUNKS) for c in pair) \
    + tuple(_PLAIN_CHUNKS[len(CONV_CHUNKS):])
ATTN_ROWS = 1024
SSD_SUB = 8
MERGE_TM = 512
FFN_TM = 1024
FFN_CW = 256
FFN_NCHUNK = D_FF // FFN_CW


def _silu_of_half(h):
    return h + h * jnp.tanh(h)


def _sigmoid(x):
    return 1.0 / (1.0 + jnp.exp(-x))


def _softplus(x):
    return jnp.maximum(x, 0.0) + jnp.log(1.0 + jnp.exp(-jnp.abs(x)))


def _tile3_heads(t):
    pad = jnp.zeros(t.shape[:-1] + (LANES - 3 * N_SSD_HEADS,), t.dtype)
    return jnp.concatenate([t, t, t, pad], axis=-1)


def _split3(x):
    hi = x.astype(BF16)
    r1 = x - hi.astype(F32)
    mid = r1.astype(BF16)
    lo = (r1 - mid.astype(F32)).astype(BF16)
    return hi, mid, lo


def _resident(shape):
    return pl.BlockSpec(shape, lambda *_: (0,) * len(shape), pipeline_mode=pl.Buffered(1))


def _conv_stage(u, buf_ref, tail_ref):
    tm = u.shape[0]
    buf_ref[0:SUBLANES, :] = tail_ref[...]
    buf_ref[SUBLANES:SUBLANES + tm, :] = u
    tail_ref[...] = u[tm - SUBLANES:tm, :]


def _conv_finish(buf_ref, w, b, taps, u=None):
    tm = buf_ref.shape[0] - SUBLANES
    if u is None:
        u = buf_ref[SUBLANES:SUBLANES + tm, :]
    acc = u * w[taps - 1:taps, :] + b
    for k in range(taps - 1):
        off = SUBLANES - (taps - 1) + k
        acc = acc + buf_ref[off:off + tm, :] * w[k:k + 1, :]
    return acc


def _inproj_kernel(x_ref, nw_ref, w_ref, cw_ref, cb_ref, wdt_ref, wdtT_ref, dtb_ref, dtbT_ref,
                   proj_ref, dt_ref, dtT_ref, buf_ref, tail_ref, *, tiles_per_seq):
    @pl.when(pl.program_id(0) % tiles_per_seq == 0)
    def _():
        tail_ref[...] = jnp.zeros(tail_ref.shape, F32)

    x = x_ref[...]
    ms = jnp.mean(x * x, axis=-1, keepdims=True)
    xn = (x * lax.rsqrt(ms + EPS) * nw_ref[...]).astype(BF16)

    def finish_conv(c):
        k = CONV_CHUNKS[c]
        ks = slice(k * IN_TN, (k + 1) * IN_TN)
        acc = _conv_finish(buf_ref.at[k % 2], cw_ref[:, ks], cb_ref[:, ks], SSD_CONV)
        proj_ref[:, c * IN_TN:(c + 1) * IN_TN] = _silu_of_half(acc).astype(BF16)

    pending = None
    for c in IN_CHUNK_ORDER:
        cs = slice(c * IN_TN, (c + 1) * IN_TN)
        u = lax.dot_general(xn, w_ref[cs, :], (((1,), (1,)), ((), ())), preferred_element_type=F32)
        if c in CONV_CHUNKS:
            k = CONV_CHUNKS[c]
            _conv_stage(u, buf_ref.at[k % 2], tail_ref.at[k])
        else:
            proj_ref[:, cs] = u.astype(BF16)
        if pending is not None:
            finish_conv(pending)
        pending = c if c in CONV_CHUNKS else None
    assert pending is None
    dt_raw = jnp.dot(xn, wdt_ref[...], preferred_element_type=F32)
    dt_ref[...] = _softplus(dt_raw + dtb_ref[...])
    dtT_raw = lax.dot_general(wdtT_ref[...], xn, (((1,), (1,)), ((), ())),
                              preferred_element_type=F32)
    dtT = _softplus(dtT_raw + dtbT_ref[...])
    for c in range(dtT_ref.shape[0]):
        dtT_ref[c] = dtT[:, c * CHUNK:(c + 1) * CHUNK]


def _inproj(x2d, norm_w, w_main, conv_w_half, conv_b_half, w_dt, w_dtT, dt_bias_row, dt_bias_col, seq):
    T = x2d.shape[0]
    tm = min(IN_TM, seq)
    return pl.pallas_call(
        functools.partial(_inproj_kernel, tiles_per_seq=seq // tm),
        grid=(T // tm,),
        in_specs=[
            pl.BlockSpec((tm, D_MODEL), lambda i: (i, 0)),
            _resident((1, D_MODEL)),
            _resident((PROJ_W, D_MODEL)),
            _resident((SSD_CONV, XBC_DIM)),
            _resident((1, XBC_DIM)),
            _resident((D_MODEL, LANES)),
            _resident((N_SSD_HEADS, D_MODEL)),
            _resident((1, LANES)),
            _resident((N_SSD_HEADS, 1)),
        ],
        out_specs=[
            pl.BlockSpec((tm, PROJ_W), lambda i: (i, 0)),
            pl.BlockSpec((tm, LANES), lambda i: (i, 0)),
            pl.BlockSpec((tm // CHUNK, N_SSD_HEADS, CHUNK), lambda i: (i, 0, 0)),
        ],
        out_shape=[
            jax.ShapeDtypeStruct((T, PROJ_W), BF16),
            jax.ShapeDtypeStruct((T, LANES), F32),
            jax.ShapeDtypeStruct((T // CHUNK, N_SSD_HEADS, CHUNK), F32),
        ],
        scratch_shapes=[
            pltpu.VMEM((2, SUBLANES + tm, IN_TN), F32),
            pltpu.VMEM((XBC_DIM // IN_TN, SUBLANES, IN_TN), F32),
        ],
        compiler_params=pltpu.CompilerParams(
            dimension_semantics=("arbitrary",), vmem_limit_bytes=VMEM_LIMIT_BYTES),
        name="inproj",
    )(x2d, norm_w, w_main, conv_w_half, conv_b_half, w_dt, w_dtT, dt_bias_row, dt_bias_col)


ATTN_HEAD_ORDER = tuple(
    (2 * (j // Q_PER_KV) + half) * Q_PER_KV + j % Q_PER_KV
    for j in range(N_Q_HEADS // 2) for half in range(2))


def _attn_kernel(sinks_ref, q_ref, kp_ref, kc_ref, vp_ref, vc_ref, o_ref):
    W = WINDOW
    nsub = q_ref.shape[0] // W
    qi = lax.broadcasted_iota(jnp.int32, (W, 2 * W), 0)
    si = lax.broadcasted_iota(jnp.int32, (W, 2 * W), 1)
    in_prev = jnp.logical_and(si < W, si > qi)
    in_cur = jnp.logical_and(si >= W, si - W <= qi)
    lane = lax.broadcasted_iota(jnp.int32, (W, LANES), 1)
    left = lane < HEAD_DIM

    def block(i, carry):
        r0 = pl.multiple_of(i * W, W)
        rows = pl.ds(r0, W)
        before = pl.ds(pl.multiple_of(jnp.maximum(r0 - W, 0), W), W)
        has_prev = jnp.logical_or(pl.program_id(1) > 0, i > 0)
        valid = jnp.logical_or(jnp.logical_and(in_prev, has_prev), in_cur)
        scores, vpairs = [], []
        for kv in range(N_KV_HEADS // 2):
            kvc = slice(kv * LANES, (kv + 1) * LANES)
            k_prev = jnp.where(i > 0, kc_ref[before, kvc], kp_ref[:, kvc])
            v_prev = jnp.where(i > 0, vc_ref[before, kvc], vp_ref[:, kvc])
            kpair = jnp.concatenate([k_prev, kc_ref[rows, kvc]], axis=0)
            vpairs.append(jnp.concatenate([v_prev, vc_ref[rows, kvc]], axis=0))
            for g in range(Q_PER_KV):
                j = kv * Q_PER_KV + g
                col = slice(j * LANES, (j + 1) * LANES)
                qc = q_ref[rows, col].astype(F32) * (HEAD_DIM ** -0.5)
                q2 = jnp.concatenate([jnp.where(left, qc, 0.0), jnp.where(left, 0.0, qc)],
                                     axis=0).astype(BF16)
                scores.append(lax.dot_general(q2, kpair, (((1,), (1,)), ((), ())),
                                              preferred_element_type=F32))
        for kv in range(N_KV_HEADS // 2):
            vpair = vpairs[kv]
            for g in range(Q_PER_KV):
                j = kv * Q_PER_KV + g
                col = slice(j * LANES, (j + 1) * LANES)
                s2 = scores[j]
                ps, rs = [], []
                for half in range(2):
                    s = jnp.where(valid, s2[half * W:(half + 1) * W, :], NEG)
                    sink = sinks_ref[2 * j + half]
                    m = jnp.maximum(jnp.max(s, axis=-1, keepdims=True), sink)
                    p = jnp.exp(s - m)
                    denom = jnp.sum(p, axis=-1, keepdims=True) + jnp.exp(sink - m)
                    ps.append(p.astype(BF16))
                    rs.append(1.0 / denom)
                o2 = jnp.dot(jnp.concatenate(ps, axis=0), vpair, preferred_element_type=F32)
                o = jnp.where(left, o2[:W, :] * rs[0], o2[W:, :] * rs[1])
                o_ref[rows, col] = o.astype(BF16)
        return carry

    lax.fori_loop(0, nsub, block, 0)


def _attention(proj, sinks_ordered, batch, seq):
    rows = min(ATTN_ROWS, seq)
    sub = rows // WINDOW
    nt = seq // rows
    T = batch * seq
    row = lambda b, n: b * nt + n
    prow = lambda b, n: (b * nt + n) * sub - jnp.minimum(n, 1)
    return pl.pallas_call(
        _attn_kernel,
        grid=(batch, nt),
        in_specs=[
            pl.BlockSpec(memory_space=pltpu.SMEM),
            pl.BlockSpec((rows, Q_DIM), lambda b, n: (row(b, n), OFF_Q // Q_DIM)),
            pl.BlockSpec((WINDOW, KV_DIM), lambda b, n: (prow(b, n), OFF_K // KV_DIM)),
            pl.BlockSpec((rows, KV_DIM), lambda b, n: (row(b, n), OFF_K // KV_DIM)),
            pl.BlockSpec((WINDOW, KV_DIM), lambda b, n: (prow(b, n), OFF_V // KV_DIM)),
            pl.BlockSpec((rows, KV_DIM), lambda b, n: (row(b, n), OFF_V // KV_DIM)),
        ],
        out_specs=pl.BlockSpec((rows, Q_DIM), lambda b, n: (row(b, n), 0)),
        out_shape=jax.ShapeDtypeStruct((T, Q_DIM), BF16),
        compiler_params=pltpu.CompilerParams(
            dimension_semantics=("arbitrary", "arbitrary"), vmem_limit_bytes=VMEM_LIMIT_BYTES),
        name="swa_attention",
    )(sinks_ordered, proj, proj, proj, proj, proj)


def _head_expand_matrix():
    k = jnp.arange(LANES)[:, None]
    c = jnp.arange(D_INNER)[None, :]
    return jnp.logical_and(k < 3 * N_SSD_HEADS, k % N_SSD_HEADS == c // SSD_HEAD_DIM).astype(BF16)


def _split3_lanes(v, lane):
    hi = v.astype(BF16).astype(F32)
    r1 = v - hi
    mid = r1.astype(BF16).astype(F32)
    parts = jnp.where(lane < N_SSD_HEADS, hi, jnp.where(lane < 2 * N_SSD_HEADS, mid, r1 - mid))
    return parts.astype(BF16)


def _ssd_kernel(xs_ref, b_ref, c_ref, dt_ref, dtT_ref,
                alog_ref, alogT_ref, dskip_ref, emat_ref,
                y_ref, state_ref):
    L = CHUNK

    @pl.when(pl.program_id(1) == 0)
    def _():
        state_ref[...] = jnp.zeros(state_ref.shape, F32)

    def chunk(i, carry):
        r0 = pl.multiple_of(i * L, L)
        rows = pl.ds(r0, L)
        dt = dt_ref[rows, :]
        dtT = dtT_ref[i]
        dA = dt * (-jnp.exp(alog_ref[...]))
        dAT = dtT * (-jnp.exp(alogT_ref[...]))

        ri = lax.broadcasted_iota(jnp.int32, (L, L), 0)
        ci = lax.broadcasted_iota(jnp.int32, (L, L), 1)
        causal = ci <= ri
        tri = jnp.where(causal, 1.0, 0.0).astype(BF16)
        triT = jnp.where(ri <= ci, 1.0, 0.0).astype(BF16)
        a_cs = sum(jnp.dot(tri, p, preferred_element_type=F32) for p in _split3(dA))
        a_csT = sum(jnp.dot(p, triT, preferred_element_type=F32) for p in _split3(dAT))

        a_last = a_cs[L - 1:L, :]
        ea = jnp.exp(a_cs)
        w_state = dt * jnp.exp(a_last - a_cs)
        a2 = a_cs * LOG2E
        a2T = (a_csT - jnp.log(dtT)) * LOG2E

        lane = lax.broadcasted_iota(jnp.int32, (L, LANES), 1)
        lane_lt_half = lane < SSD_HEAD_DIM
        ea_parts = _split3_lanes(ea, lane)
        ws_parts = _split3_lanes(w_state, lane)

        def group_inputs(g):
            gcol = slice(g * GROUP_W, (g + 1) * GROUP_W)
            ncol = slice(g * D_STATE, (g + 1) * D_STATE)
            bg_bf = b_ref[rows, ncol]
            cg_bf = c_ref[rows, ncol]
            cb = lax.dot_general(cg_bf, bg_bf, (((1,), (1,)), ((), ())), preferred_element_type=F32)
            ea_g = jnp.dot(ea_parts, emat_ref[:, gcol], preferred_element_type=F32)
            ws_g = jnp.dot(ws_parts, emat_ref[:, gcol], preferred_element_type=F32)
            y_off = jnp.dot(cg_bf, state_ref[g].astype(BF16), preferred_element_type=F32)
            return bg_bf, cb, ea_g, ws_g, y_off

        ready = group_inputs(0)
        for g in range(N_SSD_GROUPS):
            gcol = slice(g * GROUP_W, (g + 1) * GROUP_W)
            bg_bf, cb, ea_g, ws_g, y_off = ready
            if g + 1 < N_SSD_GROUPS:
                ready = group_inputs(g + 1)
            xs_bf = xs_ref[rows, gcol]
            xs = xs_bf.astype(F32)
            y_off = y_off * ea_g
            xw = (xs * ws_g).astype(BF16)
            bgT = bg_bf.astype(F32).T.astype(BF16)
            new_states = jnp.dot(bgT, xw, preferred_element_type=F32)
            state_ref[g] = state_ref[g] * ea_g[L - 1:L, :] + new_states
            y_pairs = []
            for jp in range(HEADS_PER_GROUP // 2):
                h0 = g * HEADS_PER_GROUP + 2 * jp
                xs_pair_bf = xs_bf[:, jp * LANES:(jp + 1) * LANES]
                yd = []
                for h in (h0, h0 + 1):
                    seg2 = a2[:, h:h + 1] - a2T[h:h + 1, :]
                    m = cb * jnp.exp2(jnp.where(causal, seg2, NEG))
                    yd.append(jnp.dot(m.astype(BF16), xs_pair_bf, preferred_element_type=F32))
                y_pairs.append(jnp.where(lane_lt_half, yd[0], yd[1])
                               + y_off[:, jp * LANES:(jp + 1) * LANES])
            y_ref[rows, gcol] = (jnp.concatenate(y_pairs, axis=-1) + xs * dskip_ref[:, gcol]).astype(BF16)
        return carry

    lax.fori_loop(0, xs_ref.shape[0] // L, chunk, 0)


def _ssd(proj, dt, dtT, a_log, d_skip, batch, seq):
    sub = min(SSD_SUB, seq // CHUNK)
    rows = sub * CHUNK
    nc = seq // rows
    T = batch * seq
    row = lambda b, n: b * nc + n
    alog_row = _tile3_heads(a_log.reshape(1, N_SSD_HEADS))
    alog_col = a_log.reshape(N_SSD_HEADS, 1)
    dskip_row = jnp.repeat(d_skip, SSD_HEAD_DIM).reshape(1, D_INNER)
    emat = _head_expand_matrix()
    return pl.pallas_call(
        _ssd_kernel,
        grid=(batch, nc),
        in_specs=[
            pl.BlockSpec((rows, D_INNER), lambda b, n: (row(b, n), OFF_XS // D_INNER)),
            pl.BlockSpec((rows, BC_DIM), lambda b, n: (row(b, n), OFF_B // BC_DIM)),
            pl.BlockSpec((rows, BC_DIM), lambda b, n: (row(b, n), OFF_C // BC_DIM)),
            pl.BlockSpec((rows, LANES), lambda b, n: (row(b, n), 0)),
            pl.BlockSpec((sub, N_SSD_HEADS, CHUNK), lambda b, n: (row(b, n), 0, 0)),
            _resident((1, LANES)), _resident((N_SSD_HEADS, 1)),
            _resident((1, D_INNER)),
            _resident(emat.shape),
        ],
        out_specs=pl.BlockSpec((rows, D_INNER), lambda b, n: (row(b, n), 0)),
        out_shape=jax.ShapeDtypeStruct((T, D_INNER), BF16),
        scratch_shapes=[pltpu.VMEM((N_SSD_GROUPS, D_STATE, GROUP_W), F32)],
        compiler_params=pltpu.CompilerParams(
            dimension_semantics=("arbitrary", "arbitrary"), vmem_limit_bytes=VMEM_LIMIT_BYTES),
        name="ssd_mixer",
    )(proj, proj, proj, dt, dtT, alog_row, alog_col, dskip_row, emat)


def _merge_kernel(x_ref, attn_ref, y_ref, z_ref, ga_ref, gs_ref, bg_ref, ynw_ref,
                  wa_ref, ws_ref, wo_ref, h_ref):
    attn = jnp.dot(attn_ref[...], wa_ref[...], preferred_element_type=F32)
    ssd = None
    for g in range(N_SSD_GROUPS):
        gcol = slice(g * GROUP_W, (g + 1) * GROUP_W)
        y = y_ref[:, gcol].astype(F32) * _silu_of_half(z_ref[:, gcol].astype(F32))
        ms = jnp.mean(y * y, axis=-1, keepdims=True)
        yn = (y * lax.rsqrt(ms + EPS) * ynw_ref[:, gcol]).astype(BF16)
        part = jnp.dot(yn, ws_ref[gcol, :], preferred_element_type=F32)
        ssd = part if ssd is None else ssd + part
    gate_a = _sigmoid(ga_ref[...].astype(F32) + bg_ref[:, :D_MODEL])
    gate_s = _sigmoid(gs_ref[...].astype(F32) + bg_ref[:, D_MODEL:])
    mixed = (gate_a * attn + gate_s * ssd).astype(BF16)
    h_ref[...] = x_ref[...] + jnp.dot(mixed, wo_ref[...], preferred_element_type=F32)


def _merge(x2d, attn, y, proj, b_gate, ssd_norm_w, w_attn_o, w_ssd_o, w_out):
    T = x2d.shape[0]
    tm = min(MERGE_TM, T)
    return pl.pallas_call(
        _merge_kernel,
        grid=(T // tm,),
        in_specs=[
            pl.BlockSpec((tm, D_MODEL), lambda i: (i, 0)),
            pl.BlockSpec((tm, Q_DIM), lambda i: (i, 0)),
            pl.BlockSpec((tm, D_INNER), lambda i: (i, 0)),
            pl.BlockSpec((tm, D_INNER), lambda i: (i, OFF_Z // D_INNER)),
            pl.BlockSpec((tm, D_MODEL), lambda i: (i, OFF_GA // D_MODEL)),
            pl.BlockSpec((tm, D_MODEL), lambda i: (i, OFF_GS // D_MODEL)),
            _resident((1, 2 * D_MODEL)),
            _resident((1, D_INNER)),
            _resident((Q_DIM, D_MODEL)), _resident((D_INNER, D_MODEL)), _resident((D_MODEL, D_MODEL)),
        ],
        out_specs=pl.BlockSpec((tm, D_MODEL), lambda i: (i, 0)),
        out_shape=jax.ShapeDtypeStruct((T, D_MODEL), F32),
        compiler_params=pltpu.CompilerParams(
            dimension_semantics=("arbitrary",), vmem_limit_bytes=VMEM_LIMIT_BYTES),
        name="gated_merge",
    )(x2d, attn, y, proj, proj, proj, b_gate, ssd_norm_w, w_attn_o, w_ssd_o, w_out)


def _ffn_kernel(h_ref, n2_ref, wup_ref, cw_ref, cb_ref, wdn_ref, fn_ref, o_ref,
                buf_ref, tail_ref, act_ref):
    h = h_ref[...]
    ms = jnp.mean(h * h, axis=-1, keepdims=True)
    hn = (h * lax.rsqrt(ms + EPS) * n2_ref[...]).astype(BF16)

    @pl.when(pl.program_id(1) == 0)
    def _():
        tail_ref[...] = jnp.zeros(tail_ref.shape, F32)

    def cols(c):
        return slice(c * FFN_CW, (c + 1) * FFN_CW), slice(D_FF + c * FFN_CW, D_FF + (c + 1) * FFN_CW)

    for c in range(FFN_NCHUNK):
        vs, gs = cols(c)
        u = jnp.concatenate([jnp.dot(hn, wup_ref[:, vs], preferred_element_type=F32),
                             jnp.dot(hn, wup_ref[:, gs], preferred_element_type=F32)], axis=-1)
        _conv_stage(u, buf_ref, tail_ref.at[c])
        w = jnp.concatenate([cw_ref[:, vs], cw_ref[:, gs]], axis=-1)
        b = jnp.concatenate([cb_ref[:, vs], cb_ref[:, gs]], axis=-1)
        acc = _conv_finish(buf_ref, w, b, FFN_CONV, u)
        act_ref[:, vs] = (_silu_of_half(acc[:, FFN_CW:]) * acc[:, :FFN_CW]).astype(BF16)

    h2 = h + jnp.dot(act_ref[...], wdn_ref[...], preferred_element_type=F32)
    ms2 = jnp.mean(h2 * h2, axis=-1, keepdims=True)
    o_ref[...] = h2 * lax.rsqrt(ms2 + EPS) * fn_ref[...]


def _ffn(h2d, norm2_w, w_up, conv_w, conv_b, w_down, final_w, batch, seq):
    tm = min(FFN_TM, seq)
    nt = seq // tm
    T = batch * seq
    return pl.pallas_call(
        _ffn_kernel,
        grid=(batch, nt),
        in_specs=[
            pl.BlockSpec((tm, D_MODEL), lambda b, n: (b * nt + n, 0)),
            _resident((1, D_MODEL)),
            _resident((D_MODEL, 2 * D_FF)),
            _resident((FFN_CONV, 2 * D_FF)),
            _resident((1, 2 * D_FF)),
            _resident((D_FF, D_MODEL)),
            _resident((1, D_MODEL)),
        ],
        out_specs=pl.BlockSpec((tm, D_MODEL), lambda b, n: (b * nt + n, 0)),
        out_shape=jax.ShapeDtypeStruct((T, D_MODEL), F32),
        scratch_shapes=[
            pltpu.VMEM((SUBLANES + tm, 2 * FFN_CW), F32),
            pltpu.VMEM((FFN_NCHUNK, SUBLANES, 2 * FFN_CW), F32),
            pltpu.VMEM((tm, D_FF), BF16),
        ],
        compiler_params=pltpu.CompilerParams(
            dimension_semantics=("arbitrary", "arbitrary"), vmem_limit_bytes=VMEM_LIMIT_BYTES),
        name="conv_ffn",
    )(h2d, norm2_w, w_up, conv_w, conv_b, w_down, final_w)


def _scale_gate_half(t):
    return jnp.concatenate([t[..., :D_FF], 0.5 * t[..., D_FF:]], axis=-1)


IN_O_Q = 0
IN_O_K = IN_O_Q + Q_DIM
IN_O_V = IN_O_K + KV_DIM
IN_O_Z = IN_O_V + KV_DIM
IN_O_XBC = IN_O_Z + D_INNER
IN_O_DT = IN_O_XBC + XBC_DIM
IN_O_GA = IN_O_DT + N_SSD_HEADS
IN_O_GS = IN_O_GA + D_MODEL
IN_DIM = IN_O_GS + D_MODEL


def _transposed_projection_weight(w_in):
    wT = jnp.swapaxes(w_in[0], 0, 1)
    rows = lambda a, n: wT[a:a + n]
    q = jnp.take(rows(IN_O_Q, Q_DIM).reshape(N_Q_HEADS, HEAD_DIM, D_MODEL),
                 jnp.asarray(ATTN_HEAD_ORDER, jnp.int32), axis=0).reshape(Q_DIM, D_MODEL)
    w_mainT = jnp.concatenate([
        0.5 * rows(IN_O_Z, D_INNER),
        rows(IN_O_XBC, D_INNER), q, rows(IN_O_GA, D_MODEL), rows(IN_O_GS, D_MODEL),
        rows(IN_O_XBC + D_INNER, BC_DIM), rows(IN_O_XBC + D_INNER + BC_DIM, BC_DIM),
        rows(IN_O_K, KV_DIM), rows(IN_O_V, KV_DIM)], axis=0).astype(BF16)
    return w_mainT, rows(IN_O_DT, N_SSD_HEADS)


def kernel(x, norm1_w, w_in, b_gate, attn_sinks, w_attn_o, ssd_conv_w, ssd_conv_b, dt_bias, a_log,
           d_skip, ssd_norm_w, w_ssd_o, w_out, norm2_w, w_up, ffn_conv_w, ffn_conv_b, w_down,
           final_norm_w):
    batch, seq, _ = x.shape
    T = batch * seq
    assert norm1_w.shape[0] == 1, "single-layer kernel"
    assert seq % ATTN_ROWS == 0 and seq % (SSD_SUB * CHUNK) == 0 and seq % IN_TM == 0

    head_order = jnp.asarray(ATTN_HEAD_ORDER, jnp.int32)
    w_ao = jnp.take(w_attn_o[0].reshape(N_Q_HEADS, HEAD_DIM, D_MODEL), head_order, axis=0)
    w_ao = w_ao.reshape(Q_DIM, D_MODEL).astype(BF16)
    sinks = jnp.take(attn_sinks[0], head_order)
    w_mainT, w_dtT = _transposed_projection_weight(w_in)
    w_dt_pad = _tile3_heads(w_dtT.T).astype(BF16)
    w_dtT = w_dtT.astype(BF16)
    dtb_row = _tile3_heads(dt_bias[0].reshape(1, N_SSD_HEADS))
    dtb_col = dt_bias[0].reshape(N_SSD_HEADS, 1)

    x2d = x.reshape(T, D_MODEL)
    proj, dt, dtT = _inproj(x2d, norm1_w[0].reshape(1, D_MODEL), w_mainT, 0.5 * ssd_conv_w[0],
                            0.5 * ssd_conv_b[0].reshape(1, XBC_DIM), w_dt_pad, w_dtT, dtb_row, dtb_col, seq)
    attn = _attention(proj, sinks, batch, seq)
    y = _ssd(proj, dt, dtT, a_log[0], d_skip[0], batch, seq)
    h = _merge(x2d, attn, y, proj, b_gate[0].reshape(1, 2 * D_MODEL), ssd_norm_w[0].reshape(1, D_INNER), w_ao,
               w_ssd_o[0].astype(BF16), w_out[0].astype(BF16))
    out = _ffn(h, norm2_w[0].reshape(1, D_MODEL), w_up[0].astype(BF16),
               _scale_gate_half(ffn_conv_w[0]), _scale_gate_half(ffn_conv_b[0].reshape(1, 2 * D_FF)),
               w_down[0].astype(BF16), final_norm_w.reshape(1, D_MODEL), batch, seq)
    return out.reshape(batch, seq, D_MODEL)
```

```python
import functools

import jax
import jax.numpy as jnp
from jax import lax
from jax.experimental import pallas as pl
from jax.experimental.pallas import tpu as pltpu

F32 = jnp.float32
BF16 = jnp.bfloat16

D_MODEL = 1024
N_Q_HEADS = 16
N_KV_HEADS = 4
Q_PER_KV = N_Q_HEADS // N_KV_HEADS
HEAD_DIM = 64
WINDOW = 128
D_INNER = 2048
SSD_HEAD_DIM = 64
N_SSD_HEADS = 32
N_SSD_GROUPS = 4
HEADS_PER_GROUP = N_SSD_HEADS // N_SSD_GROUPS
D_STATE = 128
SSD_CONV = 4
CHUNK = 128
D_FF = 2816
FFN_CONV = 3
EPS = 1e-5
NEG = -1e30
LOG2E = 1.4426950408889634
Q_DIM = N_Q_HEADS * HEAD_DIM
KV_DIM = N_KV_HEADS * HEAD_DIM
BC_DIM = N_SSD_GROUPS * D_STATE
XBC_DIM = D_INNER + 2 * BC_DIM
GROUP_W = D_INNER // N_SSD_GROUPS

LANES = 128
SUBLANES = 8
VMEM_LIMIT_BYTES = 56 * 1024 * 1024

OFF_Z = 0
OFF_XS = OFF_Z + D_INNER
OFF_Q = OFF_XS + D_INNER
OFF_GA = OFF_Q + Q_DIM
OFF_GS = OFF_GA + D_MODEL
OFF_B = OFF_GS + D_MODEL
OFF_C = OFF_B + BC_DIM
OFF_K = OFF_C + BC_DIM
OFF_V = OFF_K + KV_DIM
PROJ_W = OFF_V + KV_DIM

IN_TM = 512
IN_TN = 512
CONV_CHUNKS = {OFF_XS // IN_TN + k: k for k in range(D_INNER // IN_TN)}
CONV_CHUNKS[OFF_B // IN_TN] = D_INNER // IN_TN
CONV_CHUNKS[OFF_C // IN_TN] = D_INNER // IN_TN + 1
_PLAIN_CHUNKS = [c for c in range(PROJ_W // IN_TN) if c not in CONV_CHUNKS]
IN_CHUNK_ORDER = tuple(c for pair in zip(sorted(CONV_CHUNKS), _PLAIN_CHUNKS) for c in pair) \
    + tuple(_PLAIN_CHUNKS[len(CONV_CHUNKS):])
ATTN_ROWS = 2048
SSD_SUB = 16
MERGE_TM = 512
FFN_TM = 1024
FFN_CW = 256
FFN_NCHUNK = D_FF // FFN_CW


def _silu_of_half(h):
    return h + h * jnp.tanh(h)


def _sigmoid(x):
    return 1.0 / (1.0 + jnp.exp(-x))


def _softplus(x):
    return jnp.maximum(x, 0.0) + jnp.log(1.0 + jnp.exp(-jnp.abs(x)))


def _tile3_heads(t):
    pad = jnp.zeros(t.shape[:-1] + (LANES - 3 * N_SSD_HEADS,), t.dtype)
    return jnp.concatenate([t, t, t, pad], axis=-1)


def _split3(x):
    hi = x.astype(BF16)
    r1 = x - hi.astype(F32)
    mid = r1.astype(BF16)
    lo = (r1 - mid.astype(F32)).astype(BF16)
    return hi, mid, lo


def _resident(shape):
    return pl.BlockSpec(shape, lambda *_: (0,) * len(shape), pipeline_mode=pl.Buffered(1))


def _conv_stage(u, buf_ref, tail_ref):
    tm = u.shape[0]
    buf_ref[0:SUBLANES, :] = tail_ref[...]
    buf_ref[SUBLANES:SUBLANES + tm, :] = u
    tail_ref[...] = u[tm - SUBLANES:tm, :]


def _conv_finish(buf_ref, w, b, taps, u=None):
    tm = buf_ref.shape[0] - SUBLANES
    if u is None:
        u = buf_ref[SUBLANES:SUBLANES + tm, :]
    acc = u * w[taps - 1:taps, :] + b
    for k in range(taps - 1):
        off = SUBLANES - (taps - 1) + k
        acc = acc + buf_ref[off:off + tm, :] * w[k:k + 1, :]
    return acc


def _inproj_kernel(x_ref, nw_ref, w_ref, cw_ref, cb_ref, wdt_ref, wdtT_ref, dtb_ref, dtbT_ref,
                   proj_ref, dt_ref, dtT_ref, buf_ref, tail_ref, *, tiles_per_seq):
    @pl.when(pl.program_id(0) % tiles_per_seq == 0)
    def _():
        tail_ref[...] = jnp.zeros(tail_ref.shape, F32)

    x = x_ref[...]
    ms = jnp.mean(x * x, axis=-1, keepdims=True)
    xn = (x * lax.rsqrt(ms + EPS) * nw_ref[...]).astype(BF16)

    def finish_conv(c):
        k = CONV_CHUNKS[c]
        ks = slice(k * IN_TN, (k + 1) * IN_TN)
        acc = _conv_finish(buf_ref.at[k % 2], cw_ref[:, ks], cb_ref[:, ks], SSD_CONV)
        proj_ref[:, c * IN_TN:(c + 1) * IN_TN] = _silu_of_half(acc).astype(BF16)

    pending = None
    for c in IN_CHUNK_ORDER:
        cs = slice(c * IN_TN, (c + 1) * IN_TN)
        u = lax.dot_general(xn, w_ref[cs, :], (((1,), (1,)), ((), ())), preferred_element_type=F32)
        if c in CONV_CHUNKS:
            k = CONV_CHUNKS[c]
            _conv_stage(u, buf_ref.at[k % 2], tail_ref.at[k])
        else:
            proj_ref[:, cs] = u.astype(BF16)
        if pending is not None:
            finish_conv(pending)
        pending = c if c in CONV_CHUNKS else None
    assert pending is None
    dt_raw = jnp.dot(xn, wdt_ref[...], preferred_element_type=F32)
    dt_ref[...] = _softplus(dt_raw + dtb_ref[...])
    dtT_raw = lax.dot_general(wdtT_ref[...], xn, (((1,), (1,)), ((), ())),
                              preferred_element_type=F32)
    dtT = _softplus(dtT_raw + dtbT_ref[...])
    for c in range(dtT_ref.shape[0]):
        dtT_ref[c] = dtT[:, c * CHUNK:(c + 1) * CHUNK]


def _inproj(x2d, norm_w, w_main, conv_w_half, conv_b_half, w_dt, w_dtT, dt_bias_row, dt_bias_col, seq):
    T = x2d.shape[0]
    tm = min(IN_TM, seq)
    return pl.pallas_call(
        functools.partial(_inproj_kernel, tiles_per_seq=seq // tm),
        grid=(T // tm,),
        in_specs=[
            pl.BlockSpec((tm, D_MODEL), lambda i: (i, 0)),
            _resident((1, D_MODEL)),
            _resident((PROJ_W, D_MODEL)),
            _resident((SSD_CONV, XBC_DIM)),
            _resident((1, XBC_DIM)),
            _resident((D_MODEL, LANES)),
            _resident((N_SSD_HEADS, D_MODEL)),
            _resident((1, LANES)),
            _resident((N_SSD_HEADS, 1)),
        ],
        out_specs=[
            pl.BlockSpec((tm, PROJ_W), lambda i: (i, 0)),
            pl.BlockSpec((tm, LANES), lambda i: (i, 0)),
            pl.BlockSpec((tm // CHUNK, N_SSD_HEADS, CHUNK), lambda i: (i, 0, 0)),
        ],
        out_shape=[
            jax.ShapeDtypeStruct((T, PROJ_W), BF16),
            jax.ShapeDtypeStruct((T, LANES), F32),
            jax.ShapeDtypeStruct((T // CHUNK, N_SSD_HEADS, CHUNK), F32),
        ],
        scratch_shapes=[
            pltpu.VMEM((2, SUBLANES + tm, IN_TN), F32),
            pltpu.VMEM((XBC_DIM // IN_TN, SUBLANES, IN_TN), F32),
        ],
        compiler_params=pltpu.CompilerParams(
            dimension_semantics=("arbitrary",), vmem_limit_bytes=VMEM_LIMIT_BYTES),
        name="inproj",
    )(x2d, norm_w, w_main, conv_w_half, conv_b_half, w_dt, w_dtT, dt_bias_row, dt_bias_col)


ATTN_HEAD_ORDER = tuple(
    (2 * (j // Q_PER_KV) + half) * Q_PER_KV + j % Q_PER_KV
    for j in range(N_Q_HEADS // 2) for half in range(2))


def _attn_kernel(sinks_ref, q_ref, kp_ref, kc_ref, vp_ref, vc_ref, o_ref):
    W = WINDOW
    nsub = q_ref.shape[0] // W
    qi = lax.broadcasted_iota(jnp.int32, (W, 2 * W), 0)
    si = lax.broadcasted_iota(jnp.int32, (W, 2 * W), 1)
    in_prev = jnp.logical_and(si < W, si > qi)
    in_cur = jnp.logical_and(si >= W, si - W <= qi)
    lane = lax.broadcasted_iota(jnp.int32, (W, LANES), 1)
    left = lane < HEAD_DIM

    def block(i, carry):
        r0 = pl.multiple_of(i * W, W)
        rows = pl.ds(r0, W)
        before = pl.ds(pl.multiple_of(jnp.maximum(r0 - W, 0), W), W)
        has_prev = jnp.logical_or(pl.program_id(1) > 0, i > 0)
        valid = jnp.logical_or(jnp.logical_and(in_prev, has_prev), in_cur)
        scores, vpairs = [], []
        for kv in range(N_KV_HEADS // 2):
            kvc = slice(kv * LANES, (kv + 1) * LANES)
            k_prev = jnp.where(i > 0, kc_ref[before, kvc], kp_ref[:, kvc])
            v_prev = jnp.where(i > 0, vc_ref[before, kvc], vp_ref[:, kvc])
            kpair = jnp.concatenate([k_prev, kc_ref[rows, kvc]], axis=0)
            vpairs.append(jnp.concatenate([v_prev, vc_ref[rows, kvc]], axis=0))
            for g in range(Q_PER_KV):
                j = kv * Q_PER_KV + g
                col = slice(j * LANES, (j + 1) * LANES)
                qc = q_ref[rows, col].astype(F32) * (HEAD_DIM ** -0.5 * LOG2E)
                q2 = jnp.concatenate([jnp.where(left, qc, 0.0), jnp.where(left, 0.0, qc)],
                                     axis=0).astype(BF16)
                scores.append(lax.dot_general(q2, kpair, (((1,), (1,)), ((), ())),
                                              preferred_element_type=F32))
        for kv in range(N_KV_HEADS // 2):
            vpair = vpairs[kv]
            for g in range(Q_PER_KV):
                j = kv * Q_PER_KV + g
                col = slice(j * LANES, (j + 1) * LANES)
                s2 = scores[j]
                ps, rs = [], []
                for half in range(2):
                    s = jnp.where(valid, s2[half * W:(half + 1) * W, :], NEG)
                    sink = sinks_ref[2 * j + half] * LOG2E
                    m = jnp.maximum(jnp.max(s, axis=-1, keepdims=True), sink)
                    p = jnp.exp2(s - m)
                    denom = jnp.sum(p, axis=-1, keepdims=True) + jnp.exp2(sink - m)
                    ps.append(p.astype(BF16))
                    rs.append(1.0 / denom)
                o2 = jnp.dot(jnp.concatenate(ps, axis=0), vpair, preferred_element_type=F32)
                o = jnp.where(left, o2[:W, :] * rs[0], o2[W:, :] * rs[1])
                o_ref[rows, col] = o.astype(BF16)
        return carry

    lax.fori_loop(0, nsub, block, 0)


def _attention(proj, sinks_ordered, batch, seq):
    rows = min(ATTN_ROWS, seq)
    sub = rows // WINDOW
    nt = seq // rows
    T = batch * seq
    row = lambda b, n: b * nt + n
    prow = lambda b, n: (b * nt + n) * sub - jnp.minimum(n, 1)
    return pl.pallas_call(
        _attn_kernel,
        grid=(batch, nt),
        in_specs=[
            pl.BlockSpec(memory_space=pltpu.SMEM),
            pl.BlockSpec((rows, Q_DIM), lambda b, n: (row(b, n), OFF_Q // Q_DIM)),
            pl.BlockSpec((WINDOW, KV_DIM), lambda b, n: (prow(b, n), OFF_K // KV_DIM)),
            pl.BlockSpec((rows, KV_DIM), lambda b, n: (row(b, n), OFF_K // KV_DIM)),
            pl.BlockSpec((WINDOW, KV_DIM), lambda b, n: (prow(b, n), OFF_V // KV_DIM)),
            pl.BlockSpec((rows, KV_DIM), lambda b, n: (row(b, n), OFF_V // KV_DIM)),
        ],
        out_specs=pl.BlockSpec((rows, Q_DIM), lambda b, n: (row(b, n), 0)),
        out_shape=jax.ShapeDtypeStruct((T, Q_DIM), BF16),
        compiler_params=pltpu.CompilerParams(
            dimension_semantics=("arbitrary", "arbitrary"), vmem_limit_bytes=VMEM_LIMIT_BYTES),
        name="swa_attention",
    )(sinks_ordered, proj, proj, proj, proj, proj)


def _head_expand_matrix():
    k = jnp.arange(LANES)[:, None]
    c = jnp.arange(D_INNER)[None, :]
    return jnp.logical_and(k < 3 * N_SSD_HEADS, k % N_SSD_HEADS == c // SSD_HEAD_DIM).astype(BF16)


def _split3_lanes(v, lane):
    hi = v.astype(BF16).astype(F32)
    r1 = v - hi
    mid = r1.astype(BF16).astype(F32)
    parts = jnp.where(lane < N_SSD_HEADS, hi, jnp.where(lane < 2 * N_SSD_HEADS, mid, r1 - mid))
    return parts.astype(BF16)


def _ssd_kernel(xs_ref, b_ref, c_ref, dt_ref, dtT_ref,
                alog_ref, alogT_ref, dskip_ref, emat_ref,
                y_ref, state_ref):
    L = CHUNK

    @pl.when(pl.program_id(1) == 0)
    def _():
        state_ref[...] = jnp.zeros(state_ref.shape, F32)

    def chunk(i, carry):
        r0 = pl.multiple_of(i * L, L)
        rows = pl.ds(r0, L)
        dt = dt_ref[rows, :]
        dtT = dtT_ref[i]
        dA = dt * (-jnp.exp(alog_ref[...]))
        dAT = dtT * (-jnp.exp(alogT_ref[...]))

        ri = lax.broadcasted_iota(jnp.int32, (L, L), 0)
        ci = lax.broadcasted_iota(jnp.int32, (L, L), 1)
        causal = ci <= ri
        tri = jnp.where(causal, 1.0, 0.0).astype(BF16)
        triT = jnp.where(ri <= ci, 1.0, 0.0).astype(BF16)
        a_cs = sum(jnp.dot(tri, p, preferred_element_type=F32) for p in _split3(dA))
        a_csT = sum(jnp.dot(p, triT, preferred_element_type=F32) for p in _split3(dAT))

        a_last = a_cs[L - 1:L, :]
        ea = jnp.exp(a_cs)
        w_state = dt * jnp.exp(a_last - a_cs)
        a2 = a_cs * LOG2E
        a2T = (a_csT - jnp.log(dtT)) * LOG2E

        lane = lax.broadcasted_iota(jnp.int32, (L, LANES), 1)
        lane_lt_half = lane < SSD_HEAD_DIM
        ea_parts = _split3_lanes(ea, lane)
        ws_parts = _split3_lanes(w_state, lane)

        def group_inputs(g):
            gcol = slice(g * GROUP_W, (g + 1) * GROUP_W)
            ncol = slice(g * D_STATE, (g + 1) * D_STATE)
            bg_bf = b_ref[rows, ncol]
            cg_bf = c_ref[rows, ncol]
            cb = lax.dot_general(cg_bf, bg_bf, (((1,), (1,)), ((), ())), preferred_element_type=F32)
            ea_g = jnp.dot(ea_parts, emat_ref[:, gcol], preferred_element_type=F32)
            ws_g = jnp.dot(ws_parts, emat_ref[:, gcol], preferred_element_type=F32)
            y_off = jnp.dot(cg_bf, state_ref[g].astype(BF16), preferred_element_type=F32)
            return bg_bf, cb, ea_g, ws_g, y_off

        ready = group_inputs(0)
        for g in range(N_SSD_GROUPS):
            gcol = slice(g * GROUP_W, (g + 1) * GROUP_W)
            bg_bf, cb, ea_g, ws_g, y_off = ready
            if g + 1 < N_SSD_GROUPS:
                ready = group_inputs(g + 1)
            xs_bf = xs_ref[rows, gcol]
            xs = xs_bf.astype(F32)
            y_off = y_off * ea_g
            xw = (xs * ws_g).astype(BF16)
            bgT = bg_bf.astype(F32).T.astype(BF16)
            new_states = jnp.dot(bgT, xw, preferred_element_type=F32)
            state_ref[g] = state_ref[g] * ea_g[L - 1:L, :] + new_states
            y_pairs = []
            for jp in range(HEADS_PER_GROUP // 2):
                h0 = g * HEADS_PER_GROUP + 2 * jp
                xs_pair_bf = xs_bf[:, jp * LANES:(jp + 1) * LANES]
                yd = []
                for h in (h0, h0 + 1):
                    seg2 = a2[:, h:h + 1] - a2T[h:h + 1, :]
                    m = cb * jnp.exp2(jnp.where(causal, seg2, NEG))
                    yd.append(jnp.dot(m.astype(BF16), xs_pair_bf, preferred_element_type=F32))
                y_pairs.append(jnp.where(lane_lt_half, yd[0], yd[1])
                               + y_off[:, jp * LANES:(jp + 1) * LANES])
            y_ref[rows, gcol] = (jnp.concatenate(y_pairs, axis=-1) + xs * dskip_ref[:, gcol]).astype(BF16)
        return carry

    lax.fori_loop(0, xs_ref.shape[0] // L, chunk, 0)


def _ssd(proj, dt, dtT, a_log, d_skip, batch, seq):
    sub = min(SSD_SUB, seq // CHUNK)
    rows = sub * CHUNK
    nc = seq // rows
    T = batch * seq
    row = lambda b, n: b * nc + n
    alog_row = _tile3_heads(a_log.reshape(1, N_SSD_HEADS))
    alog_col = a_log.reshape(N_SSD_HEADS, 1)
    dskip_row = jnp.repeat(d_skip, SSD_HEAD_DIM).reshape(1, D_INNER)
    emat = _head_expand_matrix()
    return pl.pallas_call(
        _ssd_kernel,
        grid=(batch, nc),
        in_specs=[
            pl.BlockSpec((rows, D_INNER), lambda b, n: (row(b, n), OFF_XS // D_INNER)),
            pl.BlockSpec((rows, BC_DIM), lambda b, n: (row(b, n), OFF_B // BC_DIM)),
            pl.BlockSpec((rows, BC_DIM), lambda b, n: (row(b, n), OFF_C // BC_DIM)),
            pl.BlockSpec((rows, LANES), lambda b, n: (row(b, n), 0)),
            pl.BlockSpec((sub, N_SSD_HEADS, CHUNK), lambda b, n: (row(b, n), 0, 0)),
            _resident((1, LANES)), _resident((N_SSD_HEADS, 1)),
            _resident((1, D_INNER)),
            _resident(emat.shape),
        ],
        out_specs=pl.BlockSpec((rows, D_INNER), lambda b, n: (row(b, n), 0)),
        out_shape=jax.ShapeDtypeStruct((T, D_INNER), BF16),
        scratch_shapes=[pltpu.VMEM((N_SSD_GROUPS, D_STATE, GROUP_W), F32)],
        compiler_params=pltpu.CompilerParams(
            dimension_semantics=("arbitrary", "arbitrary"), vmem_limit_bytes=VMEM_LIMIT_BYTES),
        name="ssd_mixer",
    )(proj, proj, proj, dt, dtT, alog_row, alog_col, dskip_row, emat)


def _merge_kernel(x_ref, attn_ref, y_ref, z_ref, ga_ref, gs_ref, bg_ref, ynw_ref,
                  wa_ref, ws_ref, wo_ref, h_ref):
    attn = jnp.dot(attn_ref[...], wa_ref[...], preferred_element_type=F32)
    ssd = None
    for g in range(N_SSD_GROUPS):
        gcol = slice(g * GROUP_W, (g + 1) * GROUP_W)
        y = y_ref[:, gcol].astype(F32) * _silu_of_half(z_ref[:, gcol].astype(F32))
        ms = jnp.mean(y * y, axis=-1, keepdims=True)
        yn = (y * lax.rsqrt(ms + EPS) * ynw_ref[:, gcol]).astype(BF16)
        part = jnp.dot(yn, ws_ref[gcol, :], preferred_element_type=F32)
        ssd = part if ssd is None else ssd + part
    gate_a = _sigmoid(ga_ref[...].astype(F32) + bg_ref[:, :D_MODEL])
    gate_s = _sigmoid(gs_ref[...].astype(F32) + bg_ref[:, D_MODEL:])
    mixed = (gate_a * attn + gate_s * ssd).astype(BF16)
    h_ref[...] = x_ref[...] + jnp.dot(mixed, wo_ref[...], preferred_element_type=F32)


def _merge(x2d, attn, y, proj, b_gate, ssd_norm_w, w_attn_o, w_ssd_o, w_out):
    T = x2d.shape[0]
    tm = min(MERGE_TM, T)
    return pl.pallas_call(
        _merge_kernel,
        grid=(T // tm,),
        in_specs=[
            pl.BlockSpec((tm, D_MODEL), lambda i: (i, 0)),
            pl.BlockSpec((tm, Q_DIM), lambda i: (i, 0)),
            pl.BlockSpec((tm, D_INNER), lambda i: (i, 0)),
            pl.BlockSpec((tm, D_INNER), lambda i: (i, OFF_Z // D_INNER)),
            pl.BlockSpec((tm, D_MODEL), lambda i: (i, OFF_GA // D_MODEL)),
            pl.BlockSpec((tm, D_MODEL), lambda i: (i, OFF_GS // D_MODEL)),
            _resident((1, 2 * D_MODEL)),
            _resident((1, D_INNER)),
            _resident((Q_DIM, D_MODEL)), _resident((D_INNER, D_MODEL)), _resident((D_MODEL, D_MODEL)),
        ],
        out_specs=pl.BlockSpec((tm, D_MODEL), lambda i: (i, 0)),
        out_shape=jax.ShapeDtypeStruct((T, D_MODEL), F32),
        compiler_params=pltpu.CompilerParams(
            dimension_semantics=("arbitrary",), vmem_limit_bytes=VMEM_LIMIT_BYTES),
        name="gated_merge",
    )(x2d, attn, y, proj, proj, proj, b_gate, ssd_norm_w, w_attn_o, w_ssd_o, w_out)


def _ffn_kernel(h_ref, n2_ref, wup_ref, cw_ref, cb_ref, wdn_ref, fn_ref, o_ref,
                buf_ref, tail_ref, act_ref):
    h = h_ref[...]
    ms = jnp.mean(h * h, axis=-1, keepdims=True)
    hn = (h * lax.rsqrt(ms + EPS) * n2_ref[...]).astype(BF16)

    @pl.when(pl.program_id(1) == 0)
    def _():
        tail_ref[...] = jnp.zeros(tail_ref.shape, F32)

    def cols(c):
        return slice(c * FFN_CW, (c + 1) * FFN_CW), slice(D_FF + c * FFN_CW, D_FF + (c + 1) * FFN_CW)

    for c in range(FFN_NCHUNK):
        vs, gs = cols(c)
        u = jnp.concatenate([jnp.dot(hn, wup_ref[:, vs], preferred_element_type=F32),
                             jnp.dot(hn, wup_ref[:, gs], preferred_element_type=F32)], axis=-1)
        _conv_stage(u, buf_ref, tail_ref.at[c])
        w = jnp.concatenate([cw_ref[:, vs], cw_ref[:, gs]], axis=-1)
        b = jnp.concatenate([cb_ref[:, vs], cb_ref[:, gs]], axis=-1)
        acc = _conv_finish(buf_ref, w, b, FFN_CONV, u)
        act_ref[:, vs] = (_silu_of_half(acc[:, FFN_CW:]) * acc[:, :FFN_CW]).astype(BF16)

    h2 = h + jnp.dot(act_ref[...], wdn_ref[...], preferred_element_type=F32)
    ms2 = jnp.mean(h2 * h2, axis=-1, keepdims=True)
    o_ref[...] = h2 * lax.rsqrt(ms2 + EPS) * fn_ref[...]


def _ffn(h2d, norm2_w, w_up, conv_w, conv_b, w_down, final_w, batch, seq):
    tm = min(FFN_TM, seq)
    nt = seq // tm
    T = batch * seq
    return pl.pallas_call(
        _ffn_kernel,
        grid=(batch, nt),
        in_specs=[
            pl.BlockSpec((tm, D_MODEL), lambda b, n: (b * nt + n, 0)),
            _resident((1, D_MODEL)),
            _resident((D_MODEL, 2 * D_FF)),
            _resident((FFN_CONV, 2 * D_FF)),
            _resident((1, 2 * D_FF)),
            _resident((D_FF, D_MODEL)),
            _resident((1, D_MODEL)),
        ],
        out_specs=pl.BlockSpec((tm, D_MODEL), lambda b, n: (b * nt + n, 0)),
        out_shape=jax.ShapeDtypeStruct((T, D_MODEL), F32),
        scratch_shapes=[
            pltpu.VMEM((SUBLANES + tm, 2 * FFN_CW), F32),
            pltpu.VMEM((FFN_NCHUNK, SUBLANES, 2 * FFN_CW), F32),
            pltpu.VMEM((tm, D_FF), BF16),
        ],
        compiler_params=pltpu.CompilerParams(
            dimension_semantics=("arbitrary", "arbitrary"), vmem_limit_bytes=VMEM_LIMIT_BYTES),
        name="conv_ffn",
    )(h2d, norm2_w, w_up, conv_w, conv_b, w_down, final_w)


def _scale_gate_half(t):
    return jnp.concatenate([t[..., :D_FF], 0.5 * t[..., D_FF:]], axis=-1)


IN_O_Q = 0
IN_O_K = IN_O_Q + Q_DIM
IN_O_V = IN_O_K + KV_DIM
IN_O_Z = IN_O_V + KV_DIM
IN_O_XBC = IN_O_Z + D_INNER
IN_O_DT = IN_O_XBC + XBC_DIM
IN_O_GA = IN_O_DT + N_SSD_HEADS
IN_O_GS = IN_O_GA + D_MODEL
IN_DIM = IN_O_GS + D_MODEL


def _transposed_projection_weight(w_in):
    wT = jnp.swapaxes(w_in[0], 0, 1)
    rows = lambda a, n: wT[a:a + n]
    q = jnp.take(rows(IN_O_Q, Q_DIM).reshape(N_Q_HEADS, HEAD_DIM, D_MODEL),
                 jnp.asarray(ATTN_HEAD_ORDER, jnp.int32), axis=0).reshape(Q_DIM, D_MODEL)
    w_mainT = jnp.concatenate([
        0.5 * rows(IN_O_Z, D_INNER),
        rows(IN_O_XBC, D_INNER), q, rows(IN_O_GA, D_MODEL), rows(IN_O_GS, D_MODEL),
        rows(IN_O_XBC + D_INNER, BC_DIM), rows(IN_O_XBC + D_INNER + BC_DIM, BC_DIM),
        rows(IN_O_K, KV_DIM), rows(IN_O_V, KV_DIM)], axis=0).astype(BF16)
    return w_mainT, rows(IN_O_DT, N_SSD_HEADS)


def kernel(x, norm1_w, w_in, b_gate, attn_sinks, w_attn_o, ssd_conv_w, ssd_conv_b, dt_bias, a_log,
           d_skip, ssd_norm_w, w_ssd_o, w_out, norm2_w, w_up, ffn_conv_w, ffn_conv_b, w_down,
           final_norm_w):
    batch, seq, _ = x.shape
    T = batch * seq
    assert norm1_w.shape[0] == 1, "single-layer kernel"
    assert seq % ATTN_ROWS == 0 and seq % (SSD_SUB * CHUNK) == 0 and seq % IN_TM == 0

    head_order = jnp.asarray(ATTN_HEAD_ORDER, jnp.int32)
    w_ao = jnp.take(w_attn_o[0].reshape(N_Q_HEADS, HEAD_DIM, D_MODEL), head_order, axis=0)
    w_ao = w_ao.reshape(Q_DIM, D_MODEL).astype(BF16)
    sinks = jnp.take(attn_sinks[0], head_order)
    w_mainT, w_dtT = _transposed_projection_weight(w_in)
    w_dt_pad = _tile3_heads(w_dtT.T).astype(BF16)
    w_dtT = w_dtT.astype(BF16)
    dtb_row = _tile3_heads(dt_bias[0].reshape(1, N_SSD_HEADS))
    dtb_col = dt_bias[0].reshape(N_SSD_HEADS, 1)

    x2d = x.reshape(T, D_MODEL)
    proj, dt, dtT = _inproj(x2d, norm1_w[0].reshape(1, D_MODEL), w_mainT, 0.5 * ssd_conv_w[0],
                            0.5 * ssd_conv_b[0].reshape(1, XBC_DIM), w_dt_pad, w_dtT, dtb_row, dtb_col, seq)
    attn = _attention(proj, sinks, batch, seq)
    y = _ssd(proj, dt, dtT, a_log[0], d_skip[0], batch, seq)
    h = _merge(x2d, attn, y, proj, b_gate[0].reshape(1, 2 * D_MODEL), ssd_norm_w[0].reshape(1, D_INNER), w_ao,
               w_ssd_o[0].astype(BF16), w_out[0].astype(BF16))
    out = _ffn(h, norm2_w[0].reshape(1, D_MODEL), w_up[0].astype(BF16),
               _scale_gate_half(ffn_conv_w[0]), _scale_gate_half(ffn_conv_b[0].reshape(1, 2 * D_FF)),
               w_down[0].astype(BF16), final_norm_w.reshape(1, D_MODEL), batch, seq)
    return out.reshape(batch, seq, D_MODEL)
```

```python
import functools

import jax
import jax.numpy as jnp
from jax import lax
from jax.experimental import pallas as pl
from jax.experimental.pallas import tpu as pltpu

F32 = jnp.float32
BF16 = jnp.bfloat16

D_MODEL = 1024
N_Q_HEADS = 16
N_KV_HEADS = 4
Q_PER_KV = N_Q_HEADS // N_KV_HEADS
HEAD_DIM = 64
WINDOW = 128
D_INNER = 2048
SSD_HEAD_DIM = 64
N_SSD_HEADS = 32
N_SSD_GROUPS = 4
HEADS_PER_GROUP = N_SSD_HEADS // N_SSD_GROUPS
D_STATE = 128
SSD_CONV = 4
CHUNK = 128
D_FF = 2816
FFN_CONV = 3
EPS = 1e-5
NEG = -1e30
LOG2E = 1.4426950408889634
Q_DIM = N_Q_HEADS * HEAD_DIM
KV_DIM = N_KV_HEADS * HEAD_DIM
BC_DIM = N_SSD_GROUPS * D_STATE
XBC_DIM = D_INNER + 2 * BC_DIM
GROUP_W = D_INNER // N_SSD_GROUPS

LANES = 128
SUBLANES = 8
VMEM_LIMIT_BYTES = 56 * 1024 * 1024

OFF_Z = 0
OFF_XS = OFF_Z + D_INNER
OFF_Q = OFF_XS + D_INNER
OFF_GA = OFF_Q + Q_DIM
OFF_GS = OFF_GA + D_MODEL
OFF_B = OFF_GS + D_MODEL
OFF_C = OFF_B + BC_DIM
OFF_K = OFF_C + BC_DIM
OFF_V = OFF_K + KV_DIM
PROJ_W = OFF_V + KV_DIM

IN_TM = 512
IN_TN = 512
CONV_CHUNKS = {OFF_XS // IN_TN + k: k for k in range(D_INNER // IN_TN)}
CONV_CHUNKS[OFF_B // IN_TN] = D_INNER // IN_TN
CONV_CHUNKS[OFF_C // IN_TN] = D_INNER // IN_TN + 1
_PLAIN_CHUNKS = [c for c in range(PROJ_W // IN_TN) if c not in CONV_CHUNKS]
IN_CHUNK_ORDER = tuple(c for pair in zip(sorted(CONV_CHUNKS), _PLAIN_CHUNKS) for c in pair) \
    + tuple(_PLAIN_CHUNKS[len(CONV_CHUNKS):])
ATTN_ROWS = 2048
SSD_SUB = 8
MERGE_TM = 512
FFN_TM = 1024
FFN_CW = 256
FFN_NCHUNK = D_FF // FFN_CW


def _silu_of_half(h):
    return h + h * jnp.tanh(h)


def _sigmoid(x):
    return 1.0 / (1.0 + jnp.exp(-x))


def _softplus(x):
    return jnp.maximum(x, 0.0) + jnp.log(1.0 + jnp.exp(-jnp.abs(x)))


def _tile3_heads(t):
    pad = jnp.zeros(t.shape[:-1] + (LANES - 3 * N_SSD_HEADS,), t.dtype)
    return jnp.concatenate([t, t, t, pad], axis=-1)


def _split3(x):
    hi = x.astype(BF16)
    r1 = x - hi.astype(F32)
    mid = r1.astype(BF16)
    lo = (r1 - mid.astype(F32)).astype(BF16)
    return hi, mid, lo


def _resident(shape):
    return pl.BlockSpec(shape, lambda *_: (0,) * len(shape), pipeline_mode=pl.Buffered(1))


def _conv_stage(u, buf_ref, tail_ref):
    tm = u.shape[0]
    buf_ref[0:SUBLANES, :] = tail_ref[...]
    buf_ref[SUBLANES:SUBLANES + tm, :] = u
    tail_ref[...] = u[tm - SUBLANES:tm, :]


def _conv_finish(buf_ref, w, b, taps, u=None):
    tm = buf_ref.shape[0] - SUBLANES
    if u is None:
        u = buf_ref[SUBLANES:SUBLANES + tm, :]
    acc = u * w[taps - 1:taps, :] + b
    for k in range(taps - 1):
        off = SUBLANES - (taps - 1) + k
        acc = acc + buf_ref[off:off + tm, :] * w[k:k + 1, :]
    return acc


def _inproj_kernel(x_ref, nw_ref, w_ref, cw_ref, cb_ref, wdt_ref, wdtT_ref, dtb_ref, dtbT_ref,
                   proj_ref, dt_ref, dtT_ref, buf_ref, tail_ref, *, tiles_per_seq):
    @pl.when(pl.program_id(0) % tiles_per_seq == 0)
    def _():
        tail_ref[...] = jnp.zeros(tail_ref.shape, F32)

    x = x_ref[...]
    ms = jnp.mean(x * x, axis=-1, keepdims=True)
    xn = (x * lax.rsqrt(ms + EPS) * nw_ref[...]).astype(BF16)

    def finish_conv(c):
        k = CONV_CHUNKS[c]
        ks = slice(k * IN_TN, (k + 1) * IN_TN)
        acc = _conv_finish(buf_ref.at[k % 2], cw_ref[:, ks], cb_ref[:, ks], SSD_CONV)
        proj_ref[:, c * IN_TN:(c + 1) * IN_TN] = _silu_of_half(acc).astype(BF16)

    dt_raw = jnp.dot(xn, wdt_ref[...], preferred_element_type=F32)
    dt_ref[...] = _softplus(dt_raw + dtb_ref[...])
    dtT_raw = lax.dot_general(wdtT_ref[...], xn, (((1,), (1,)), ((), ())),
                              preferred_element_type=F32)
    dtT = _softplus(dtT_raw + dtbT_ref[...])
    for c in range(dtT_ref.shape[0]):
        dtT_ref[c] = dtT[:, c * CHUNK:(c + 1) * CHUNK]

    pending = None
    for c in IN_CHUNK_ORDER:
        cs = slice(c * IN_TN, (c + 1) * IN_TN)
        u = lax.dot_general(xn, w_ref[cs, :], (((1,), (1,)), ((), ())), preferred_element_type=F32)
        if c in CONV_CHUNKS:
            k = CONV_CHUNKS[c]
            _conv_stage(u, buf_ref.at[k % 2], tail_ref.at[k])
        else:
            proj_ref[:, cs] = u.astype(BF16)
        if pending is not None:
            finish_conv(pending)
        pending = c if c in CONV_CHUNKS else None
    assert pending is None


def _inproj(x2d, norm_w, w_main, conv_w_half, conv_b_half, w_dt, w_dtT, dt_bias_row, dt_bias_col, seq):
    T = x2d.shape[0]
    tm = min(IN_TM, seq)
    return pl.pallas_call(
        functools.partial(_inproj_kernel, tiles_per_seq=seq // tm),
        grid=(T // tm,),
        in_specs=[
            pl.BlockSpec((tm, D_MODEL), lambda i: (i, 0)),
            _resident((1, D_MODEL)),
            _resident((PROJ_W, D_MODEL)),
            _resident((SSD_CONV, XBC_DIM)),
            _resident((1, XBC_DIM)),
            _resident((D_MODEL, LANES)),
            _resident((N_SSD_HEADS, D_MODEL)),
            _resident((1, LANES)),
            _resident((N_SSD_HEADS, 1)),
        ],
        out_specs=[
            pl.BlockSpec((tm, PROJ_W), lambda i: (i, 0)),
            pl.BlockSpec((tm, LANES), lambda i: (i, 0)),
            pl.BlockSpec((tm // CHUNK, N_SSD_HEADS, CHUNK), lambda i: (i, 0, 0)),
        ],
        out_shape=[
            jax.ShapeDtypeStruct((T, PROJ_W), BF16),
            jax.ShapeDtypeStruct((T, LANES), F32),
            jax.ShapeDtypeStruct((T // CHUNK, N_SSD_HEADS, CHUNK), F32),
        ],
        scratch_shapes=[
            pltpu.VMEM((2, SUBLANES + tm, IN_TN), F32),
            pltpu.VMEM((XBC_DIM // IN_TN, SUBLANES, IN_TN), F32),
        ],
        compiler_params=pltpu.CompilerParams(
            dimension_semantics=("arbitrary",), vmem_limit_bytes=VMEM_LIMIT_BYTES),
        name="inproj",
    )(x2d, norm_w, w_main, conv_w_half, conv_b_half, w_dt, w_dtT, dt_bias_row, dt_bias_col)


ATTN_HEAD_ORDER = tuple(
    (2 * (j // Q_PER_KV) + half) * Q_PER_KV + j % Q_PER_KV
    for j in range(N_Q_HEADS // 2) for half in range(2))


def _attn_kernel(sinks_ref, q_ref, kp_ref, kc_ref, vp_ref, vc_ref, o_ref):
    W = WINDOW
    nsub = q_ref.shape[0] // W
    qi = lax.broadcasted_iota(jnp.int32, (W, 2 * W), 0)
    si = lax.broadcasted_iota(jnp.int32, (W, 2 * W), 1)
    in_prev = jnp.logical_and(si < W, si > qi)
    in_cur = jnp.logical_and(si >= W, si - W <= qi)
    lane = lax.broadcasted_iota(jnp.int32, (W, LANES), 1)
    left = lane < HEAD_DIM

    def block(i, carry):
        r0 = pl.multiple_of(i * W, W)
        rows = pl.ds(r0, W)
        before = pl.ds(pl.multiple_of(jnp.maximum(r0 - W, 0), W), W)
        has_prev = jnp.logical_or(pl.program_id(1) > 0, i > 0)
        valid = jnp.logical_or(jnp.logical_and(in_prev, has_prev), in_cur)
        scores, vpairs = [], []
        for kv in range(N_KV_HEADS // 2):
            kvc = slice(kv * LANES, (kv + 1) * LANES)
            k_prev = jnp.where(i > 0, kc_ref[before, kvc], kp_ref[:, kvc])
            v_prev = jnp.where(i > 0, vc_ref[before, kvc], vp_ref[:, kvc])
            kpair = jnp.concatenate([k_prev, kc_ref[rows, kvc]], axis=0)
            vpairs.append(jnp.concatenate([v_prev, vc_ref[rows, kvc]], axis=0))
            for g in range(Q_PER_KV):
                j = kv * Q_PER_KV + g
                col = slice(j * LANES, (j + 1) * LANES)
                qc = q_ref[rows, col].astype(F32) * (HEAD_DIM ** -0.5 * LOG2E)
                q2 = jnp.concatenate([jnp.where(left, qc, 0.0), jnp.where(left, 0.0, qc)],
                                     axis=0).astype(BF16)
                scores.append(lax.dot_general(q2, kpair, (((1,), (1,)), ((), ())),
                                              preferred_element_type=F32))
        for kv in range(N_KV_HEADS // 2):
            vpair = vpairs[kv]
            for g in range(Q_PER_KV):
                j = kv * Q_PER_KV + g
                col = slice(j * LANES, (j + 1) * LANES)
                s2 = scores[j]
                ps, rs = [], []
                for half in range(2):
                    s = jnp.where(valid, s2[half * W:(half + 1) * W, :], NEG)
                    sink = sinks_ref[2 * j + half] * LOG2E
                    m = jnp.maximum(jnp.max(s, axis=-1, keepdims=True), sink)
                    p = jnp.exp2(s - m)
                    denom = jnp.sum(p, axis=-1, keepdims=True) + jnp.exp2(sink - m)
                    ps.append(p.astype(BF16))
                    rs.append(1.0 / denom)
                o2 = jnp.dot(jnp.concatenate(ps, axis=0), vpair, preferred_element_type=F32)
                o = jnp.where(left, o2[:W, :] * rs[0], o2[W:, :] * rs[1])
                o_ref[rows, col] = o.astype(BF16)
        return carry

    lax.fori_loop(0, nsub, block, 0)


def _attention(proj, sinks_ordered, batch, seq):
    rows = min(ATTN_ROWS, seq)
    sub = rows // WINDOW
    nt = seq // rows
    T = batch * seq
    row = lambda b, n: b * nt + n
    prow = lambda b, n: (b * nt + n) * sub - jnp.minimum(n, 1)
    return pl.pallas_call(
        _attn_kernel,
        grid=(batch, nt),
        in_specs=[
            pl.BlockSpec(memory_space=pltpu.SMEM),
            pl.BlockSpec((rows, Q_DIM), lambda b, n: (row(b, n), OFF_Q // Q_DIM)),
            pl.BlockSpec((WINDOW, KV_DIM), lambda b, n: (prow(b, n), OFF_K // KV_DIM)),
            pl.BlockSpec((rows, KV_DIM), lambda b, n: (row(b, n), OFF_K // KV_DIM)),
            pl.BlockSpec((WINDOW, KV_DIM), lambda b, n: (prow(b, n), OFF_V // KV_DIM)),
            pl.BlockSpec((rows, KV_DIM), lambda b, n: (row(b, n), OFF_V // KV_DIM)),
        ],
        out_specs=pl.BlockSpec((rows, Q_DIM), lambda b, n: (row(b, n), 0)),
        out_shape=jax.ShapeDtypeStruct((T, Q_DIM), BF16),
        compiler_params=pltpu.CompilerParams(
            dimension_semantics=("arbitrary", "arbitrary"), vmem_limit_bytes=VMEM_LIMIT_BYTES),
        name="swa_attention",
    )(sinks_ordered, proj, proj, proj, proj, proj)


def _head_expand_matrix():
    k = jnp.arange(LANES)[:, None]
    c = jnp.arange(D_INNER)[None, :]
    return jnp.logical_and(k < 3 * N_SSD_HEADS, k % N_SSD_HEADS == c // SSD_HEAD_DIM).astype(BF16)


def _split3_lanes(v, lane):
    hi = v.astype(BF16).astype(F32)
    r1 = v - hi
    mid = r1.astype(BF16).astype(F32)
    parts = jnp.where(lane < N_SSD_HEADS, hi, jnp.where(lane < 2 * N_SSD_HEADS, mid, r1 - mid))
    return parts.astype(BF16)


def _ssd_kernel(xs_ref, b_ref, c_ref, dt_ref, dtT_ref,
                alog_ref, alogT_ref, dskip_ref, emat_ref,
                y_ref, state_ref):
    L = CHUNK

    @pl.when(pl.program_id(1) == 0)
    def _():
        state_ref[...] = jnp.zeros(state_ref.shape, F32)

    def chunk(i, carry):
        r0 = pl.multiple_of(i * L, L)
        rows = pl.ds(r0, L)
        dt = dt_ref[rows, :]
        dtT = dtT_ref[i]
        dA = dt * (-jnp.exp(alog_ref[...]))
        dAT = dtT * (-jnp.exp(alogT_ref[...]))

        ri = lax.broadcasted_iota(jnp.int32, (L, L), 0)
        ci = lax.broadcasted_iota(jnp.int32, (L, L), 1)
        causal = ci <= ri
        tri = jnp.where(causal, 1.0, 0.0).astype(BF16)
        triT = jnp.where(ri <= ci, 1.0, 0.0).astype(BF16)
        a_cs = sum(jnp.dot(tri, p, preferred_element_type=F32) for p in _split3(dA))
        a_csT = sum(jnp.dot(p, triT, preferred_element_type=F32) for p in _split3(dAT))

        a_last = a_cs[L - 1:L, :]
        ea = jnp.exp(a_cs)
        w_state = dt * jnp.exp(a_last - a_cs)
        a2 = a_cs * LOG2E
        a2T = (a_csT - jnp.log(dtT)) * LOG2E

        lane = lax.broadcasted_iota(jnp.int32, (L, LANES), 1)
        lane_lt_half = lane < SSD_HEAD_DIM
        ea_parts = _split3_lanes(ea, lane)
        ws_parts = _split3_lanes(w_state, lane)

        def group_inputs(g):
            gcol = slice(g * GROUP_W, (g + 1) * GROUP_W)
            ncol = slice(g * D_STATE, (g + 1) * D_STATE)
            bg_bf = b_ref[rows, ncol]
            cg_bf = c_ref[rows, ncol]
            cb = lax.dot_general(cg_bf, bg_bf, (((1,), (1,)), ((), ())), preferred_element_type=F32)
            ea_g = jnp.dot(ea_parts, emat_ref[:, gcol], preferred_element_type=F32)
            ws_g = jnp.dot(ws_parts, emat_ref[:, gcol], preferred_element_type=F32)
            y_off = jnp.dot(cg_bf, state_ref[g].astype(BF16), preferred_element_type=F32)
            return bg_bf, cb, ea_g, ws_g, y_off

        ready = group_inputs(0)
        for g in range(N_SSD_GROUPS):
            gcol = slice(g * GROUP_W, (g + 1) * GROUP_W)
            bg_bf, cb, ea_g, ws_g, y_off = ready
            if g + 1 < N_SSD_GROUPS:
                ready = group_inputs(g + 1)
            xs_bf = xs_ref[rows, gcol]
            xs = xs_bf.astype(F32)
            y_off = y_off * ea_g
            xw = (xs * ws_g).astype(BF16)
            bgT = bg_bf.astype(F32).T.astype(BF16)
            new_states = jnp.dot(bgT, xw, preferred_element_type=F32)
            state_ref[g] = state_ref[g] * ea_g[L - 1:L, :] + new_states
            y_pairs = []
            for jp in range(HEADS_PER_GROUP // 2):
                h0 = g * HEADS_PER_GROUP + 2 * jp
                xs_pair_bf = xs_bf[:, jp * LANES:(jp + 1) * LANES]
                yd = []
                for h in (h0, h0 + 1):
                    seg2 = a2[:, h:h + 1] - a2T[h:h + 1, :]
                    m = cb * jnp.exp2(jnp.where(causal, seg2, NEG))
                    yd.append(jnp.dot(m.astype(BF16), xs_pair_bf, preferred_element_type=F32))
                y_pairs.append(jnp.where(lane_lt_half, yd[0], yd[1])
                               + y_off[:, jp * LANES:(jp + 1) * LANES])
            y_ref[rows, gcol] = (jnp.concatenate(y_pairs, axis=-1) + xs * dskip_ref[:, gcol]).astype(BF16)
        return carry

    lax.fori_loop(0, xs_ref.shape[0] // L, chunk, 0)


def _ssd(proj, dt, dtT, a_log, d_skip, batch, seq):
    sub = min(SSD_SUB, seq // CHUNK)
    rows = sub * CHUNK
    nc = seq // rows
    T = batch * seq
    row = lambda b, n: b * nc + n
    alog_row = _tile3_heads(a_log.reshape(1, N_SSD_HEADS))
    alog_col = a_log.reshape(N_SSD_HEADS, 1)
    dskip_row = jnp.repeat(d_skip, SSD_HEAD_DIM).reshape(1, D_INNER)
    emat = _head_expand_matrix()
    return pl.pallas_call(
        _ssd_kernel,
        grid=(batch, nc),
        in_specs=[
            pl.BlockSpec((rows, D_INNER), lambda b, n: (row(b, n), OFF_XS // D_INNER)),
            pl.BlockSpec((rows, BC_DIM), lambda b, n: (row(b, n), OFF_B // BC_DIM)),
            pl.BlockSpec((rows, BC_DIM), lambda b, n: (row(b, n), OFF_C // BC_DIM)),
            pl.BlockSpec((rows, LANES), lambda b, n: (row(b, n), 0)),
            pl.BlockSpec((sub, N_SSD_HEADS, CHUNK), lambda b, n: (row(b, n), 0, 0)),
            _resident((1, LANES)), _resident((N_SSD_HEADS, 1)),
            _resident((1, D_INNER)),
            _resident(emat.shape),
        ],
        out_specs=pl.BlockSpec((rows, D_INNER), lambda b, n: (row(b, n), 0)),
        out_shape=jax.ShapeDtypeStruct((T, D_INNER), BF16),
        scratch_shapes=[pltpu.VMEM((N_SSD_GROUPS, D_STATE, GROUP_W), F32)],
        compiler_params=pltpu.CompilerParams(
            dimension_semantics=("arbitrary", "arbitrary"), vmem_limit_bytes=VMEM_LIMIT_BYTES),
        name="ssd_mixer",
    )(proj, proj, proj, dt, dtT, alog_row, alog_col, dskip_row, emat)


def _merge_kernel(x_ref, attn_ref, y_ref, z_ref, ga_ref, gs_ref, bg_ref, ynw_ref,
                  wa_ref, ws_ref, wo_ref, h_ref):
    attn = jnp.dot(attn_ref[...], wa_ref[...], preferred_element_type=F32)
    ssd = None
    for g in range(N_SSD_GROUPS):
        gcol = slice(g * GROUP_W, (g + 1) * GROUP_W)
        y = y_ref[:, gcol].astype(F32) * _silu_of_half(z_ref[:, gcol].astype(F32))
        ms = jnp.mean(y * y, axis=-1, keepdims=True)
        yn = (y * lax.rsqrt(ms + EPS) * ynw_ref[:, gcol]).astype(BF16)
        part = jnp.dot(yn, ws_ref[gcol, :], preferred_element_type=F32)
        ssd = part if ssd is None else ssd + part
    gate_a = _sigmoid(ga_ref[...].astype(F32) + bg_ref[:, :D_MODEL])
    gate_s = _sigmoid(gs_ref[...].astype(F32) + bg_ref[:, D_MODEL:])
    mixed = (gate_a * attn + gate_s * ssd).astype(BF16)
    h_ref[...] = x_ref[...] + jnp.dot(mixed, wo_ref[...], preferred_element_type=F32)


def _merge(x2d, attn, y, proj, b_gate, ssd_norm_w, w_attn_o, w_ssd_o, w_out):
    T = x2d.shape[0]
    tm = min(MERGE_TM, T)
    return pl.pallas_call(
        _merge_kernel,
        grid=(T // tm,),
        in_specs=[
            pl.BlockSpec((tm, D_MODEL), lambda i: (i, 0)),
            pl.BlockSpec((tm, Q_DIM), lambda i: (i, 0)),
            pl.BlockSpec((tm, D_INNER), lambda i: (i, 0)),
            pl.BlockSpec((tm, D_INNER), lambda i: (i, OFF_Z // D_INNER)),
            pl.BlockSpec((tm, D_MODEL), lambda i: (i, OFF_GA // D_MODEL)),
            pl.BlockSpec((tm, D_MODEL), lambda i: (i, OFF_GS // D_MODEL)),
            _resident((1, 2 * D_MODEL)),
            _resident((1, D_INNER)),
            _resident((Q_DIM, D_MODEL)), _resident((D_INNER, D_MODEL)), _resident((D_MODEL, D_MODEL)),
        ],
        out_specs=pl.BlockSpec((tm, D_MODEL), lambda i: (i, 0)),
        out_shape=jax.ShapeDtypeStruct((T, D_MODEL), F32),
        compiler_params=pltpu.CompilerParams(
            dimension_semantics=("arbitrary",), vmem_limit_bytes=VMEM_LIMIT_BYTES),
        name="gated_merge",
    )(x2d, attn, y, proj, proj, proj, b_gate, ssd_norm_w, w_attn_o, w_ssd_o, w_out)


def _ffn_kernel(h_ref, n2_ref, wup_ref, cw_ref, cb_ref, wdn_ref, fn_ref, o_ref,
                buf_ref, tail_ref, act_ref):
    h = h_ref[...]
    ms = jnp.mean(h * h, axis=-1, keepdims=True)
    hn = (h * lax.rsqrt(ms + EPS) * n2_ref[...]).astype(BF16)

    @pl.when(pl.program_id(1) == 0)
    def _():
        tail_ref[...] = jnp.zeros(tail_ref.shape, F32)

    def cols(c):
        return slice(c * FFN_CW, (c + 1) * FFN_CW), slice(D_FF + c * FFN_CW, D_FF + (c + 1) * FFN_CW)

    for c in range(FFN_NCHUNK):
        vs, gs = cols(c)
        u = jnp.concatenate([jnp.dot(hn, wup_ref[:, vs], preferred_element_type=F32),
                             jnp.dot(hn, wup_ref[:, gs], preferred_element_type=F32)], axis=-1)
        _conv_stage(u, buf_ref, tail_ref.at[c])
        w = jnp.concatenate([cw_ref[:, vs], cw_ref[:, gs]], axis=-1)
        b = jnp.concatenate([cb_ref[:, vs], cb_ref[:, gs]], axis=-1)
        acc = _conv_finish(buf_ref, w, b, FFN_CONV, u)
        act_ref[:, vs] = (_silu_of_half(acc[:, FFN_CW:]) * acc[:, :FFN_CW]).astype(BF16)

    h2 = h + jnp.dot(act_ref[...], wdn_ref[...], preferred_element_type=F32)
    ms2 = jnp.mean(h2 * h2, axis=-1, keepdims=True)
    o_ref[...] = h2 * lax.rsqrt(ms2 + EPS) * fn_ref[...]


def _ffn(h2d, norm2_w, w_up, conv_w, conv_b, w_down, final_w, batch, seq):
    tm = min(FFN_TM, seq)
    nt = seq // tm
    T = batch * seq
    return pl.pallas_call(
        _ffn_kernel,
        grid=(batch, nt),
        in_specs=[
            pl.BlockSpec((tm, D_MODEL), lambda b, n: (b * nt + n, 0)),
            _resident((1, D_MODEL)),
            _resident((D_MODEL, 2 * D_FF)),
            _resident((FFN_CONV, 2 * D_FF)),
            _resident((1, 2 * D_FF)),
            _resident((D_FF, D_MODEL)),
            _resident((1, D_MODEL)),
        ],
        out_specs=pl.BlockSpec((tm, D_MODEL), lambda b, n: (b * nt + n, 0)),
        out_shape=jax.ShapeDtypeStruct((T, D_MODEL), F32),
        scratch_shapes=[
            pltpu.VMEM((SUBLANES + tm, 2 * FFN_CW), F32),
            pltpu.VMEM((FFN_NCHUNK, SUBLANES, 2 * FFN_CW), F32),
            pltpu.VMEM((tm, D_FF), BF16),
        ],
        compiler_params=pltpu.CompilerParams(
            dimension_semantics=("arbitrary", "arbitrary"), vmem_limit_bytes=VMEM_LIMIT_BYTES),
        name="conv_ffn",
    )(h2d, norm2_w, w_up, conv_w, conv_b, w_down, final_w)


def _scale_gate_half(t):
    return jnp.concatenate([t[..., :D_FF], 0.5 * t[..., D_FF:]], axis=-1)


IN_O_Q = 0
IN_O_K = IN_O_Q + Q_DIM
IN_O_V = IN_O_K + KV_DIM
IN_O_Z = IN_O_V + KV_DIM
IN_O_XBC = IN_O_Z + D_INNER
IN_O_DT = IN_O_XBC + XBC_DIM
IN_O_GA = IN_O_DT + N_SSD_HEADS
IN_O_GS = IN_O_GA + D_MODEL
IN_DIM = IN_O_GS + D_MODEL


def _transposed_projection_weight(w_in):
    wT = jnp.swapaxes(w_in[0], 0, 1)
    rows = lambda a, n: wT[a:a + n]
    q = jnp.take(rows(IN_O_Q, Q_DIM).reshape(N_Q_HEADS, HEAD_DIM, D_MODEL),
                 jnp.asarray(ATTN_HEAD_ORDER, jnp.int32), axis=0).reshape(Q_DIM, D_MODEL)
    w_mainT = jnp.concatenate([
        0.5 * rows(IN_O_Z, D_INNER),
        rows(IN_O_XBC, D_INNER), q, rows(IN_O_GA, D_MODEL), rows(IN_O_GS, D_MODEL),
        rows(IN_O_XBC + D_INNER, BC_DIM), rows(IN_O_XBC + D_INNER + BC_DIM, BC_DIM),
        rows(IN_O_K, KV_DIM), rows(IN_O_V, KV_DIM)], axis=0).astype(BF16)
    return w_mainT, rows(IN_O_DT, N_SSD_HEADS)


def kernel(x, norm1_w, w_in, b_gate, attn_sinks, w_attn_o, ssd_conv_w, ssd_conv_b, dt_bias, a_log,
           d_skip, ssd_norm_w, w_ssd_o, w_out, norm2_w, w_up, ffn_conv_w, ffn_conv_b, w_down,
           final_norm_w):
    batch, seq, _ = x.shape
    T = batch * seq
    assert norm1_w.shape[0] == 1, "single-layer kernel"
    assert seq % ATTN_ROWS == 0 and seq % (SSD_SUB * CHUNK) == 0 and seq % IN_TM == 0

    head_order = jnp.asarray(ATTN_HEAD_ORDER, jnp.int32)
    w_ao = jnp.take(w_attn_o[0].reshape(N_Q_HEADS, HEAD_DIM, D_MODEL), head_order, axis=0)
    w_ao = w_ao.reshape(Q_DIM, D_MODEL).astype(BF16)
    sinks = jnp.take(attn_sinks[0], head_order)
    w_mainT, w_dtT = _transposed_projection_weight(w_in)
    w_dt_pad = _tile3_heads(w_dtT.T).astype(BF16)
    w_dtT = w_dtT.astype(BF16)
    dtb_row = _tile3_heads(dt_bias[0].reshape(1, N_SSD_HEADS))
    dtb_col = dt_bias[0].reshape(N_SSD_HEADS, 1)

    x2d = x.reshape(T, D_MODEL)
    proj, dt, dtT = _inproj(x2d, norm1_w[0].reshape(1, D_MODEL), w_mainT, 0.5 * ssd_conv_w[0],
                            0.5 * ssd_conv_b[0].reshape(1, XBC_DIM), w_dt_pad, w_dtT, dtb_row, dtb_col, seq)
    attn = _attention(proj, sinks, batch, seq)
    y = _ssd(proj, dt, dtT, a_log[0], d_skip[0], batch, seq)
    h = _merge(x2d, attn, y, proj, b_gate[0].reshape(1, 2 * D_MODEL), ssd_norm_w[0].reshape(1, D_INNER), w_ao,
               w_ssd_o[0].astype(BF16), w_out[0].astype(BF16))
    out = _ffn(h, norm2_w[0].reshape(1, D_MODEL), w_up[0].astype(BF16),
               _scale_gate_half(ffn_conv_w[0]), _scale_gate_half(ffn_conv_b[0].reshape(1, 2 * D_FF)),
               w_down[0].astype(BF16), final_norm_w.reshape(1, D_MODEL), batch, seq)
    return out.reshape(batch, seq, D_MODEL)
```

```python
import functools

import jax
import jax.numpy as jnp
from jax import lax
from jax.experimental import pallas as pl
from jax.experimental.pallas import tpu as pltpu

F32 = jnp.float32
BF16 = jnp.bfloat16

D_MODEL = 1024
N_Q_HEADS = 16
N_KV_HEADS = 4
Q_PER_KV = N_Q_HEADS // N_KV_HEADS
HEAD_DIM = 64
WINDOW = 128
D_INNER = 2048
SSD_HEAD_DIM = 64
N_SSD_HEADS = 32
N_SSD_GROUPS = 4
HEADS_PER_GROUP = N_SSD_HEADS // N_SSD_GROUPS
D_STATE = 128
SSD_CONV = 4
CHUNK = 128
D_FF = 2816
FFN_CONV = 3
EPS = 1e-5
NEG = -1e30
LOG2E = 1.4426950408889634
Q_DIM = N_Q_HEADS * HEAD_DIM
KV_DIM = N_KV_HEADS * HEAD_DIM
BC_DIM = N_SSD_GROUPS * D_STATE
XBC_DIM = D_INNER + 2 * BC_DIM
GROUP_W = D_INNER // N_SSD_GROUPS

LANES = 128
SUBLANES = 8
VMEM_LIMIT_BYTES = 56 * 1024 * 1024

OFF_Z = 0
OFF_XS = OFF_Z + D_INNER
OFF_Q = OFF_XS + D_INNER
OFF_GA = OFF_Q + Q_DIM
OFF_GS = OFF_GA + D_MODEL
OFF_B = OFF_GS + D_MODEL
OFF_C = OFF_B + BC_DIM
OFF_K = OFF_C + BC_DIM
OFF_V = OFF_K + KV_DIM
PROJ_W = OFF_V + KV_DIM

IN_TM = 512
IN_TN = 512
CONV_CHUNKS = {OFF_XS // IN_TN + k: k for k in range(D_INNER // IN_TN)}
CONV_CHUNKS[OFF_B // IN_TN] = D_INNER // IN_TN
CONV_CHUNKS[OFF_C // IN_TN] = D_INNER // IN_TN + 1
_PLAIN_CHUNKS = [c for c in range(PROJ_W // IN_TN) if c not in CONV_CHUNKS]
IN_CHUNK_ORDER = tuple(c for pair in zip(sorted(CONV_CHUNKS), _PLAIN_CHUNKS) for c in pair) \
    + tuple(_PLAIN_CHUNKS[len(CONV_CHUNKS):])
ATTN_ROWS = 2048
SSD_SUB = 8
MERGE_TM = 512
MERGE_FFN_TM = 512
MERGE_FFN_VMEM_BYTES = 62 * 1024 * 1024
FFN_TM = 1024
FFN_CW = 256
FFN_NCHUNK = D_FF // FFN_CW


def _silu_of_half(h):
    return h + h * jnp.tanh(h)


def _sigmoid(x):
    return 1.0 / (1.0 + jnp.exp(-x))


def _softplus(x):
    return jnp.maximum(x, 0.0) + jnp.log(1.0 + jnp.exp(-jnp.abs(x)))


def _tile3_heads(t):
    pad = jnp.zeros(t.shape[:-1] + (LANES - 3 * N_SSD_HEADS,), t.dtype)
    return jnp.concatenate([t, t, t, pad], axis=-1)


def _split3(x):
    hi = x.astype(BF16)
    r1 = x - hi.astype(F32)
    mid = r1.astype(BF16)
    lo = (r1 - mid.astype(F32)).astype(BF16)
    return hi, mid, lo


def _resident(shape):
    return pl.BlockSpec(shape, lambda *_: (0,) * len(shape), pipeline_mode=pl.Buffered(1))


def _conv_stage(u, buf_ref, tail_ref):
    tm = u.shape[0]
    buf_ref[0:SUBLANES, :] = tail_ref[...]
    buf_ref[SUBLANES:SUBLANES + tm, :] = u
    tail_ref[...] = u[tm - SUBLANES:tm, :]


def _conv_finish(buf_ref, w, b, taps, u=None):
    tm = buf_ref.shape[0] - SUBLANES
    if u is None:
        u = buf_ref[SUBLANES:SUBLANES + tm, :]
    acc = u * w[taps - 1:taps, :] + b
    for k in range(taps - 1):
        off = SUBLANES - (taps - 1) + k
        acc = acc + buf_ref[off:off + tm, :] * w[k:k + 1, :]
    return acc


def _inproj_kernel(x_ref, nw_ref, w_ref, cw_ref, cb_ref, wdt_ref, wdtT_ref, dtb_ref, dtbT_ref,
                   proj_ref, dt_ref, dtT_ref, buf_ref, tail_ref, *, tiles_per_seq):
    @pl.when(pl.program_id(0) % tiles_per_seq == 0)
    def _():
        tail_ref[...] = jnp.zeros(tail_ref.shape, F32)

    x = x_ref[...]
    ms = jnp.mean(x * x, axis=-1, keepdims=True)
    xn = (x * lax.rsqrt(ms + EPS) * nw_ref[...]).astype(BF16)

    def finish_conv(c):
        k = CONV_CHUNKS[c]
        ks = slice(k * IN_TN, (k + 1) * IN_TN)
        acc = _conv_finish(buf_ref.at[k % 2], cw_ref[:, ks], cb_ref[:, ks], SSD_CONV)
        proj_ref[:, c * IN_TN:(c + 1) * IN_TN] = _silu_of_half(acc).astype(BF16)

    dt_raw = jnp.dot(xn, wdt_ref[...], preferred_element_type=F32)
    dt_ref[...] = _softplus(dt_raw + dtb_ref[...])
    dtT_raw = lax.dot_general(wdtT_ref[...], xn, (((1,), (1,)), ((), ())),
                              preferred_element_type=F32)
    dtT = _softplus(dtT_raw + dtbT_ref[...])
    for c in range(dtT_ref.shape[0]):
        dtT_ref[c] = dtT[:, c * CHUNK:(c + 1) * CHUNK]

    pending = None
    for c in IN_CHUNK_ORDER:
        cs = slice(c * IN_TN, (c + 1) * IN_TN)
        u = lax.dot_general(xn, w_ref[cs, :], (((1,), (1,)), ((), ())), preferred_element_type=F32)
        if c in CONV_CHUNKS:
            k = CONV_CHUNKS[c]
            _conv_stage(u, buf_ref.at[k % 2], tail_ref.at[k])
        else:
            proj_ref[:, cs] = u.astype(BF16)
        if pending is not None:
            finish_conv(pending)
        pending = c if c in CONV_CHUNKS else None
    assert pending is None


def _inproj(x2d, norm_w, w_main, conv_w_half, conv_b_half, w_dt, w_dtT, dt_bias_row, dt_bias_col, seq):
    T = x2d.shape[0]
    tm = min(IN_TM, seq)
    return pl.pallas_call(
        functools.partial(_inproj_kernel, tiles_per_seq=seq // tm),
        grid=(T // tm,),
        in_specs=[
            pl.BlockSpec((tm, D_MODEL), lambda i: (i, 0)),
            _resident((1, D_MODEL)),
            _resident((PROJ_W, D_MODEL)),
            _resident((SSD_CONV, XBC_DIM)),
            _resident((1, XBC_DIM)),
            _resident((D_MODEL, LANES)),
            _resident((N_SSD_HEADS, D_MODEL)),
            _resident((1, LANES)),
            _resident((N_SSD_HEADS, 1)),
        ],
        out_specs=[
            pl.BlockSpec((tm, PROJ_W), lambda i: (i, 0)),
            pl.BlockSpec((tm, LANES), lambda i: (i, 0)),
            pl.BlockSpec((tm // CHUNK, N_SSD_HEADS, CHUNK), lambda i: (i, 0, 0)),
        ],
        out_shape=[
            jax.ShapeDtypeStruct((T, PROJ_W), BF16),
            jax.ShapeDtypeStruct((T, LANES), F32),
            jax.ShapeDtypeStruct((T // CHUNK, N_SSD_HEADS, CHUNK), F32),
        ],
        scratch_shapes=[
            pltpu.VMEM((2, SUBLANES + tm, IN_TN), F32),
            pltpu.VMEM((XBC_DIM // IN_TN, SUBLANES, IN_TN), F32),
        ],
        compiler_params=pltpu.CompilerParams(
            dimension_semantics=("arbitrary",), vmem_limit_bytes=VMEM_LIMIT_BYTES),
        name="inproj",
    )(x2d, norm_w, w_main, conv_w_half, conv_b_half, w_dt, w_dtT, dt_bias_row, dt_bias_col)


ATTN_HEAD_ORDER = tuple(
    (2 * (j // Q_PER_KV) + half) * Q_PER_KV + j % Q_PER_KV
    for j in range(N_Q_HEADS // 2) for half in range(2))


def _attn_kernel(sinks_ref, q_ref, kp_ref, kc_ref, vp_ref, vc_ref, o_ref):
    W = WINDOW
    nsub = q_ref.shape[0] // W
    qi = lax.broadcasted_iota(jnp.int32, (W, 2 * W), 0)
    si = lax.broadcasted_iota(jnp.int32, (W, 2 * W), 1)
    in_prev = jnp.logical_and(si < W, si > qi)
    in_cur = jnp.logical_and(si >= W, si - W <= qi)
    lane = lax.broadcasted_iota(jnp.int32, (W, LANES), 1)
    left = lane < HEAD_DIM

    def block(i, carry):
        r0 = pl.multiple_of(i * W, W)
        rows = pl.ds(r0, W)
        before = pl.ds(pl.multiple_of(jnp.maximum(r0 - W, 0), W), W)
        has_prev = jnp.logical_or(pl.program_id(1) > 0, i > 0)
        valid = jnp.logical_or(jnp.logical_and(in_prev, has_prev), in_cur)
        scores, vpairs = [], []
        for kv in range(N_KV_HEADS // 2):
            kvc = slice(kv * LANES, (kv + 1) * LANES)
            k_prev = jnp.where(i > 0, kc_ref[before, kvc], kp_ref[:, kvc])
            v_prev = jnp.where(i > 0, vc_ref[before, kvc], vp_ref[:, kvc])
            kpair = jnp.concatenate([k_prev, kc_ref[rows, kvc]], axis=0)
            vpairs.append(jnp.concatenate([v_prev, vc_ref[rows, kvc]], axis=0))
            for g in range(Q_PER_KV):
                j = kv * Q_PER_KV + g
                col = slice(j * LANES, (j + 1) * LANES)
                qc = q_ref[rows, col].astype(F32) * (HEAD_DIM ** -0.5 * LOG2E)
                q2 = jnp.concatenate([jnp.where(left, qc, 0.0), jnp.where(left, 0.0, qc)],
                                     axis=0).astype(BF16)
                scores.append(lax.dot_general(q2, kpair, (((1,), (1,)), ((), ())),
                                              preferred_element_type=F32))
        for kv in range(N_KV_HEADS // 2):
            vpair = vpairs[kv]
            for g in range(Q_PER_KV):
                j = kv * Q_PER_KV + g
                col = slice(j * LANES, (j + 1) * LANES)
                s2 = scores[j]
                ps, rs = [], []
                for half in range(2):
                    s = jnp.where(valid, s2[half * W:(half + 1) * W, :], NEG)
                    sink = sinks_ref[2 * j + half] * LOG2E
                    m = jnp.maximum(jnp.max(s, axis=-1, keepdims=True), sink)
                    p = jnp.exp2(s - m)
                    denom = jnp.sum(p, axis=-1, keepdims=True) + jnp.exp2(sink - m)
                    ps.append(p.astype(BF16))
                    rs.append(1.0 / denom)
                o2 = jnp.dot(jnp.concatenate(ps, axis=0), vpair, preferred_element_type=F32)
                o = jnp.where(left, o2[:W, :] * rs[0], o2[W:, :] * rs[1])
                o_ref[rows, col] = o.astype(BF16)
        return carry

    lax.fori_loop(0, nsub, block, 0)


def _attention(proj, sinks_ordered, batch, seq):
    rows = min(ATTN_ROWS, seq)
    sub = rows // WINDOW
    nt = seq // rows
    T = batch * seq
    row = lambda b, n: b * nt + n
    prow = lambda b, n: (b * nt + n) * sub - jnp.minimum(n, 1)
    return pl.pallas_call(
        _attn_kernel,
        grid=(batch, nt),
        in_specs=[
            pl.BlockSpec(memory_space=pltpu.SMEM),
            pl.BlockSpec((rows, Q_DIM), lambda b, n: (row(b, n), OFF_Q // Q_DIM)),
            pl.BlockSpec((WINDOW, KV_DIM), lambda b, n: (prow(b, n), OFF_K // KV_DIM)),
            pl.BlockSpec((rows, KV_DIM), lambda b, n: (row(b, n), OFF_K // KV_DIM)),
            pl.BlockSpec((WINDOW, KV_DIM), lambda b, n: (prow(b, n), OFF_V // KV_DIM)),
            pl.BlockSpec((rows, KV_DIM), lambda b, n: (row(b, n), OFF_V // KV_DIM)),
        ],
        out_specs=pl.BlockSpec((rows, Q_DIM), lambda b, n: (row(b, n), 0)),
        out_shape=jax.ShapeDtypeStruct((T, Q_DIM), BF16),
        compiler_params=pltpu.CompilerParams(
            dimension_semantics=("arbitrary", "arbitrary"), vmem_limit_bytes=VMEM_LIMIT_BYTES),
        name="swa_attention",
    )(sinks_ordered, proj, proj, proj, proj, proj)


def _head_expand_matrix():
    k = jnp.arange(LANES)[:, None]
    c = jnp.arange(D_INNER)[None, :]
    return jnp.logical_and(k < 3 * N_SSD_HEADS, k % N_SSD_HEADS == c // SSD_HEAD_DIM).astype(BF16)


def _split3_lanes(v, lane):
    hi = v.astype(BF16).astype(F32)
    r1 = v - hi
    mid = r1.astype(BF16).astype(F32)
    parts = jnp.where(lane < N_SSD_HEADS, hi, jnp.where(lane < 2 * N_SSD_HEADS, mid, r1 - mid))
    return parts.astype(BF16)


def _ssd_kernel(xs_ref, b_ref, c_ref, dt_ref, dtT_ref,
                alog_ref, alogT_ref, dskip_ref, emat_ref,
                y_ref, state_ref):
    L = CHUNK

    @pl.when(pl.program_id(1) == 0)
    def _():
        state_ref[...] = jnp.zeros(state_ref.shape, F32)

    def chunk(i, carry):
        r0 = pl.multiple_of(i * L, L)
        rows = pl.ds(r0, L)
        dt = dt_ref[rows, :]
        dtT = dtT_ref[i]
        dA = dt * (-jnp.exp(alog_ref[...]))
        dAT = dtT * (-jnp.exp(alogT_ref[...]))

        ri = lax.broadcasted_iota(jnp.int32, (L, L), 0)
        ci = lax.broadcasted_iota(jnp.int32, (L, L), 1)
        causal = ci <= ri
        tri = jnp.where(causal, 1.0, 0.0).astype(BF16)
        triT = jnp.where(ri <= ci, 1.0, 0.0).astype(BF16)
        a_cs = sum(jnp.dot(tri, p, preferred_element_type=F32) for p in _split3(dA))
        a_csT = sum(jnp.dot(p, triT, preferred_element_type=F32) for p in _split3(dAT))

        a_last = a_cs[L - 1:L, :]
        ea = jnp.exp(a_cs)
        w_state = dt * jnp.exp(a_last - a_cs)
        a2 = a_cs * LOG2E
        a2T = (a_csT - jnp.log(dtT)) * LOG2E

        lane = lax.broadcasted_iota(jnp.int32, (L, LANES), 1)
        lane_lt_half = lane < SSD_HEAD_DIM
        ea_parts = _split3_lanes(ea, lane)
        ws_parts = _split3_lanes(w_state, lane)

        def group_inputs(g):
            gcol = slice(g * GROUP_W, (g + 1) * GROUP_W)
            ncol = slice(g * D_STATE, (g + 1) * D_STATE)
            bg_bf = b_ref[rows, ncol]
            cg_bf = c_ref[rows, ncol]
            cb = lax.dot_general(cg_bf, bg_bf, (((1,), (1,)), ((), ())), preferred_element_type=F32)
            ea_g = jnp.dot(ea_parts, emat_ref[:, gcol], preferred_element_type=F32)
            ws_g = jnp.dot(ws_parts, emat_ref[:, gcol], preferred_element_type=F32)
            y_off = jnp.dot(cg_bf, state_ref[g].astype(BF16), preferred_element_type=F32)
            return bg_bf, cb, ea_g, ws_g, y_off

        ready = group_inputs(0)
        for g in range(N_SSD_GROUPS):
            gcol = slice(g * GROUP_W, (g + 1) * GROUP_W)
            bg_bf, cb, ea_g, ws_g, y_off = ready
            if g + 1 < N_SSD_GROUPS:
                ready = group_inputs(g + 1)
            xs_bf = xs_ref[rows, gcol]
            xs = xs_bf.astype(F32)
            y_off = y_off * ea_g
            xw = (xs * ws_g).astype(BF16)
            bgT = bg_bf.astype(F32).T.astype(BF16)
            new_states = jnp.dot(bgT, xw, preferred_element_type=F32)
            state_ref[g] = state_ref[g] * ea_g[L - 1:L, :] + new_states
            y_pairs = []
            for jp in range(HEADS_PER_GROUP // 2):
                h0 = g * HEADS_PER_GROUP + 2 * jp
                xs_pair_bf = xs_bf[:, jp * LANES:(jp + 1) * LANES]
                yd = []
                for h in (h0, h0 + 1):
                    seg2 = a2[:, h:h + 1] - a2T[h:h + 1, :]
                    m = cb * jnp.exp2(jnp.where(causal, seg2, NEG))
                    yd.append(jnp.dot(m.astype(BF16), xs_pair_bf, preferred_element_type=F32))
                y_pairs.append(jnp.where(lane_lt_half, yd[0], yd[1])
                               + y_off[:, jp * LANES:(jp + 1) * LANES])
            y_ref[rows, gcol] = (jnp.concatenate(y_pairs, axis=-1) + xs * dskip_ref[:, gcol]).astype(BF16)
        return carry

    lax.fori_loop(0, xs_ref.shape[0] // L, chunk, 0)


def _ssd(proj, dt, dtT, a_log, d_skip, batch, seq):
    sub = min(SSD_SUB, seq // CHUNK)
    rows = sub * CHUNK
    nc = seq // rows
    T = batch * seq
    row = lambda b, n: b * nc + n
    alog_row = _tile3_heads(a_log.reshape(1, N_SSD_HEADS))
    alog_col = a_log.reshape(N_SSD_HEADS, 1)
    dskip_row = jnp.repeat(d_skip, SSD_HEAD_DIM).reshape(1, D_INNER)
    emat = _head_expand_matrix()
    return pl.pallas_call(
        _ssd_kernel,
        grid=(batch, nc),
        in_specs=[
            pl.BlockSpec((rows, D_INNER), lambda b, n: (row(b, n), OFF_XS // D_INNER)),
            pl.BlockSpec((rows, BC_DIM), lambda b, n: (row(b, n), OFF_B // BC_DIM)),
            pl.BlockSpec((rows, BC_DIM), lambda b, n: (row(b, n), OFF_C // BC_DIM)),
            pl.BlockSpec((rows, LANES), lambda b, n: (row(b, n), 0)),
            pl.BlockSpec((sub, N_SSD_HEADS, CHUNK), lambda b, n: (row(b, n), 0, 0)),
            _resident((1, LANES)), _resident((N_SSD_HEADS, 1)),
            _resident((1, D_INNER)),
            _resident(emat.shape),
        ],
        out_specs=pl.BlockSpec((rows, D_INNER), lambda b, n: (row(b, n), 0)),
        out_shape=jax.ShapeDtypeStruct((T, D_INNER), BF16),
        scratch_shapes=[pltpu.VMEM((N_SSD_GROUPS, D_STATE, GROUP_W), F32)],
        compiler_params=pltpu.CompilerParams(
            dimension_semantics=("arbitrary", "arbitrary"), vmem_limit_bytes=VMEM_LIMIT_BYTES),
        name="ssd_mixer",
    )(proj, proj, proj, dt, dtT, alog_row, alog_col, dskip_row, emat)


def _merge_kernel(x_ref, attn_ref, y_ref, z_ref, ga_ref, gs_ref, bg_ref, ynw_ref,
                  wa_ref, ws_ref, wo_ref, h_ref):
    attn = jnp.dot(attn_ref[...], wa_ref[...], preferred_element_type=F32)
    ssd = None
    for g in range(N_SSD_GROUPS):
        gcol = slice(g * GROUP_W, (g + 1) * GROUP_W)
        y = y_ref[:, gcol].astype(F32) * _silu_of_half(z_ref[:, gcol].astype(F32))
        ms = jnp.mean(y * y, axis=-1, keepdims=True)
        yn = (y * lax.rsqrt(ms + EPS) * ynw_ref[:, gcol]).astype(BF16)
        part = jnp.dot(yn, ws_ref[gcol, :], preferred_element_type=F32)
        ssd = part if ssd is None else ssd + part
    gate_a = _sigmoid(ga_ref[...].astype(F32) + bg_ref[:, :D_MODEL])
    gate_s = _sigmoid(gs_ref[...].astype(F32) + bg_ref[:, D_MODEL:])
    mixed = (gate_a * attn + gate_s * ssd).astype(BF16)
    h_ref[...] = x_ref[...] + jnp.dot(mixed, wo_ref[...], preferred_element_type=F32)


def _merge(x2d, attn, y, proj, b_gate, ssd_norm_w, w_attn_o, w_ssd_o, w_out):
    T = x2d.shape[0]
    tm = min(MERGE_TM, T)
    return pl.pallas_call(
        _merge_kernel,
        grid=(T // tm,),
        in_specs=[
            pl.BlockSpec((tm, D_MODEL), lambda i: (i, 0)),
            pl.BlockSpec((tm, Q_DIM), lambda i: (i, 0)),
            pl.BlockSpec((tm, D_INNER), lambda i: (i, 0)),
            pl.BlockSpec((tm, D_INNER), lambda i: (i, OFF_Z // D_INNER)),
            pl.BlockSpec((tm, D_MODEL), lambda i: (i, OFF_GA // D_MODEL)),
            pl.BlockSpec((tm, D_MODEL), lambda i: (i, OFF_GS // D_MODEL)),
            _resident((1, 2 * D_MODEL)),
            _resident((1, D_INNER)),
            _resident((Q_DIM, D_MODEL)), _resident((D_INNER, D_MODEL)), _resident((D_MODEL, D_MODEL)),
        ],
        out_specs=pl.BlockSpec((tm, D_MODEL), lambda i: (i, 0)),
        out_shape=jax.ShapeDtypeStruct((T, D_MODEL), F32),
        compiler_params=pltpu.CompilerParams(
            dimension_semantics=("arbitrary",), vmem_limit_bytes=VMEM_LIMIT_BYTES),
        name="gated_merge",
    )(x2d, attn, y, proj, proj, proj, b_gate, ssd_norm_w, w_attn_o, w_ssd_o, w_out)


def _ffn_kernel(h_ref, n2_ref, wup_ref, cw_ref, cb_ref, wdn_ref, fn_ref, o_ref,
                buf_ref, tail_ref, act_ref):
    h = h_ref[...]
    ms = jnp.mean(h * h, axis=-1, keepdims=True)
    hn = (h * lax.rsqrt(ms + EPS) * n2_ref[...]).astype(BF16)

    @pl.when(pl.program_id(1) == 0)
    def _():
        tail_ref[...] = jnp.zeros(tail_ref.shape, F32)

    def cols(c):
        return slice(c * FFN_CW, (c + 1) * FFN_CW), slice(D_FF + c * FFN_CW, D_FF + (c + 1) * FFN_CW)

    for c in range(FFN_NCHUNK):
        vs, gs = cols(c)
        u = jnp.concatenate([jnp.dot(hn, wup_ref[:, vs], preferred_element_type=F32),
                             jnp.dot(hn, wup_ref[:, gs], preferred_element_type=F32)], axis=-1)
        _conv_stage(u, buf_ref, tail_ref.at[c])
        w = jnp.concatenate([cw_ref[:, vs], cw_ref[:, gs]], axis=-1)
        b = jnp.concatenate([cb_ref[:, vs], cb_ref[:, gs]], axis=-1)
        acc = _conv_finish(buf_ref, w, b, FFN_CONV, u)
        act_ref[:, vs] = (_silu_of_half(acc[:, FFN_CW:]) * acc[:, :FFN_CW]).astype(BF16)

    h2 = h + jnp.dot(act_ref[...], wdn_ref[...], preferred_element_type=F32)
    ms2 = jnp.mean(h2 * h2, axis=-1, keepdims=True)
    o_ref[...] = h2 * lax.rsqrt(ms2 + EPS) * fn_ref[...]


def _ffn(h2d, norm2_w, w_up, conv_w, conv_b, w_down, final_w, batch, seq):
    tm = min(FFN_TM, seq)
    nt = seq // tm
    T = batch * seq
    return pl.pallas_call(
        _ffn_kernel,
        grid=(batch, nt),
        in_specs=[
            pl.BlockSpec((tm, D_MODEL), lambda b, n: (b * nt + n, 0)),
            _resident((1, D_MODEL)),
            _resident((D_MODEL, 2 * D_FF)),
            _resident((FFN_CONV, 2 * D_FF)),
            _resident((1, 2 * D_FF)),
            _resident((D_FF, D_MODEL)),
            _resident((1, D_MODEL)),
        ],
        out_specs=pl.BlockSpec((tm, D_MODEL), lambda b, n: (b * nt + n, 0)),
        out_shape=jax.ShapeDtypeStruct((T, D_MODEL), F32),
        scratch_shapes=[
            pltpu.VMEM((SUBLANES + tm, 2 * FFN_CW), F32),
            pltpu.VMEM((FFN_NCHUNK, SUBLANES, 2 * FFN_CW), F32),
            pltpu.VMEM((tm, D_FF), BF16),
        ],
        compiler_params=pltpu.CompilerParams(
            dimension_semantics=("arbitrary", "arbitrary"), vmem_limit_bytes=VMEM_LIMIT_BYTES),
        name="conv_ffn",
    )(h2d, norm2_w, w_up, conv_w, conv_b, w_down, final_w)


def _merge_ffn_kernel(x_ref, attn_ref, y_ref, z_ref, ga_ref, gs_ref, bg_ref, ynw_ref, wa_ref, ws_ref, wo_ref,
                      n2_ref, wup_ref, cw_ref, cb_ref, wdn_ref, fn_ref, o_ref,
                      h_ref, buf_ref, tail_ref, act_ref):
    _merge_kernel(x_ref, attn_ref, y_ref, z_ref, ga_ref, gs_ref, bg_ref, ynw_ref, wa_ref, ws_ref, wo_ref, h_ref)
    _ffn_kernel(h_ref, n2_ref, wup_ref, cw_ref, cb_ref, wdn_ref, fn_ref, o_ref, buf_ref, tail_ref, act_ref)


def _merge_ffn(x2d, attn, y, proj, b_gate, ssd_norm_w, w_attn_o, w_ssd_o, w_out,
               norm2_w, w_up, conv_w, conv_b, w_down, final_w, batch, seq):
    tm = min(MERGE_FFN_TM, seq)
    nt = seq // tm
    T = batch * seq
    row = lambda b, n: b * nt + n
    return pl.pallas_call(
        _merge_ffn_kernel,
        grid=(batch, nt),
        in_specs=[
            pl.BlockSpec((tm, D_MODEL), lambda b, n: (row(b, n), 0)),
            pl.BlockSpec((tm, Q_DIM), lambda b, n: (row(b, n), 0)),
            pl.BlockSpec((tm, D_INNER), lambda b, n: (row(b, n), 0)),
            pl.BlockSpec((tm, D_INNER), lambda b, n: (row(b, n), OFF_Z // D_INNER)),
            pl.BlockSpec((tm, D_MODEL), lambda b, n: (row(b, n), OFF_GA // D_MODEL)),
            pl.BlockSpec((tm, D_MODEL), lambda b, n: (row(b, n), OFF_GS // D_MODEL)),
            _resident((1, 2 * D_MODEL)),
            _resident((1, D_INNER)),
            _resident((Q_DIM, D_MODEL)), _resident((D_INNER, D_MODEL)), _resident((D_MODEL, D_MODEL)),
            _resident((1, D_MODEL)),
            _resident((D_MODEL, 2 * D_FF)),
            _resident((FFN_CONV, 2 * D_FF)),
            _resident((1, 2 * D_FF)),
            _resident((D_FF, D_MODEL)),
            _resident((1, D_MODEL)),
        ],
        out_specs=pl.BlockSpec((tm, D_MODEL), lambda b, n: (row(b, n), 0)),
        out_shape=jax.ShapeDtypeStruct((T, D_MODEL), F32),
        scratch_shapes=[
            pltpu.VMEM((tm, D_MODEL), F32),
            pltpu.VMEM((SUBLANES + tm, 2 * FFN_CW), F32),
            pltpu.VMEM((FFN_NCHUNK, SUBLANES, 2 * FFN_CW), F32),
            pltpu.VMEM((tm, D_FF), BF16),
        ],
        compiler_params=pltpu.CompilerParams(
            dimension_semantics=("arbitrary", "arbitrary"), vmem_limit_bytes=MERGE_FFN_VMEM_BYTES),
        name="merge_ffn",
    )(x2d, attn, y, proj, proj, proj, b_gate, ssd_norm_w, w_attn_o, w_ssd_o, w_out,
      norm2_w, w_up, conv_w, conv_b, w_down, final_w)


def _scale_gate_half(t):
    return jnp.concatenate([t[..., :D_FF], 0.5 * t[..., D_FF:]], axis=-1)


IN_O_Q = 0
IN_O_K = IN_O_Q + Q_DIM
IN_O_V = IN_O_K + KV_DIM
IN_O_Z = IN_O_V + KV_DIM
IN_O_XBC = IN_O_Z + D_INNER
IN_O_DT = IN_O_XBC + XBC_DIM
IN_O_GA = IN_O_DT + N_SSD_HEADS
IN_O_GS = IN_O_GA + D_MODEL
IN_DIM = IN_O_GS + D_MODEL


def _transposed_projection_weight(w_in):
    wT = jnp.swapaxes(w_in[0], 0, 1)
    rows = lambda a, n: wT[a:a + n]
    q = jnp.take(rows(IN_O_Q, Q_DIM).reshape(N_Q_HEADS, HEAD_DIM, D_MODEL),
                 jnp.asarray(ATTN_HEAD_ORDER, jnp.int32), axis=0).reshape(Q_DIM, D_MODEL)
    w_mainT = jnp.concatenate([
        0.5 * rows(IN_O_Z, D_INNER),
        rows(IN_O_XBC, D_INNER), q, rows(IN_O_GA, D_MODEL), rows(IN_O_GS, D_MODEL),
        rows(IN_O_XBC + D_INNER, BC_DIM), rows(IN_O_XBC + D_INNER + BC_DIM, BC_DIM),
        rows(IN_O_K, KV_DIM), rows(IN_O_V, KV_DIM)], axis=0).astype(BF16)
    return w_mainT, rows(IN_O_DT, N_SSD_HEADS)


def kernel(x, norm1_w, w_in, b_gate, attn_sinks, w_attn_o, ssd_conv_w, ssd_conv_b, dt_bias, a_log,
           d_skip, ssd_norm_w, w_ssd_o, w_out, norm2_w, w_up, ffn_conv_w, ffn_conv_b, w_down,
           final_norm_w):
    batch, seq, _ = x.shape
    T = batch * seq
    assert norm1_w.shape[0] == 1, "single-layer kernel"
    assert seq % ATTN_ROWS == 0 and seq % (SSD_SUB * CHUNK) == 0 and seq % IN_TM == 0

    head_order = jnp.asarray(ATTN_HEAD_ORDER, jnp.int32)
    w_ao = jnp.take(w_attn_o[0].reshape(N_Q_HEADS, HEAD_DIM, D_MODEL), head_order, axis=0)
    w_ao = w_ao.reshape(Q_DIM, D_MODEL).astype(BF16)
    sinks = jnp.take(attn_sinks[0], head_order)
    w_mainT, w_dtT = _transposed_projection_weight(w_in)
    w_dt_pad = _tile3_heads(w_dtT.T).astype(BF16)
    w_dtT = w_dtT.astype(BF16)
    dtb_row = _tile3_heads(dt_bias[0].reshape(1, N_SSD_HEADS))
    dtb_col = dt_bias[0].reshape(N_SSD_HEADS, 1)

    x2d = x.reshape(T, D_MODEL)
    proj, dt, dtT = _inproj(x2d, norm1_w[0].reshape(1, D_MODEL), w_mainT, 0.5 * ssd_conv_w[0],
                            0.5 * ssd_conv_b[0].reshape(1, XBC_DIM), w_dt_pad, w_dtT, dtb_row, dtb_col, seq)
    attn = _attention(proj, sinks, batch, seq)
    y = _ssd(proj, dt, dtT, a_log[0], d_skip[0], batch, seq)
    out = _merge_ffn(x2d, attn, y, proj, b_gate[0].reshape(1, 2 * D_MODEL), ssd_norm_w[0].reshape(1, D_INNER),
                     w_ao, w_ssd_o[0].astype(BF16), w_out[0].astype(BF16),
                     norm2_w[0].reshape(1, D_MODEL), w_up[0].astype(BF16),
                     _scale_gate_half(ffn_conv_w[0]), _scale_gate_half(ffn_conv_b[0].reshape(1, 2 * D_FF)),
                     w_down[0].astype(BF16), final_norm_w.reshape(1, D_MODEL), batch, seq)
    return out.reshape(batch, seq, D_MODEL)
```
